```python
import jax, jax.numpy as jnp
from jax import lax
import numpy as np

D_MODEL = 1024
BATCH = 2
SEQ = 16384
DEPTH = 4
DEC_BATCH = 8
DEC_SEQ = 8192
PAST_LEN = 128

M_HEADS = 4
M_DQK = 128
M_DV = 256
M_CHUNK = 128
D_HEADS = 8
D_DK = 128
D_DV = 128
D_CHUNK = 64
C_WIDTH = D_MODEL
CONV_W = 3
D_FF = 2816
EPS = 1e-6

M_QK = M_HEADS * M_DQK
M_V = M_HEADS * M_DV
D_QK = D_HEADS * D_DK
D_V = D_HEADS * D_DV
COLS = (M_QK, M_QK, M_V, M_V, 4 * M_HEADS, 2 * D_QK + D_V, D_V, 4 * D_HEADS, 3 * C_WIDTH, 3 * D_MODEL)
PROJ_WIDTH = 2 * M_QK + 2 * M_V + 4 * M_HEADS + 2 * D_QK + 2 * D_V + 4 * D_HEADS + 3 * C_WIDTH + 3 * D_MODEL

kernel_name = 'hybrid_bidir_mlstm_gdn_conv_encoder'


def rmsnorm(x, g):
    xf = x.astype(jnp.float32)
    y = xf * lax.rsqrt(jnp.mean(xf * xf, axis=-1, keepdims=True) + EPS)
    return (y * g.astype(jnp.float32)).astype(x.dtype)


def l2norm(x):
    return x * lax.rsqrt(jnp.sum(x * x, axis=-1, keepdims=True) + EPS)


def dwconv3(x, w):
    S = x.shape[-2]
    xp = jnp.pad(x, [(0, 0)] * (x.ndim - 2) + [(1, 1), (0, 0)])
    return xp[..., :S, :] * w[0] + xp[..., 1:S + 1, :] * w[1] + xp[..., 2:, :] * w[2]


def mlstm_chunked(q, k, v, ig, lf):
    B, H, S, DK = q.shape
    DV = v.shape[-1]
    L = M_CHUNK
    N = S // L
    to_chunks = lambda t: jnp.moveaxis(t.reshape(B, H, N, L, *t.shape[3:]), 2, 0)
    xs = tuple(to_chunks(t) for t in (q, k, v, ig, lf))
    causal = jnp.tril(jnp.ones((L, L), dtype=bool))

    def step(carry, inp):
        C, n, m = carry
        q_, k_, v_, i_, f_ = inp
        b = jnp.cumsum(f_, axis=-1)
        a = b + m[..., None]
        dmat = jnp.where(causal, b[..., :, None] - b[..., None, :] + i_[..., None, :], -jnp.inf)
        mt = jnp.maximum(a, jnp.max(dmat, axis=-1))
        s = jnp.einsum('bhld,bhsd->bhls', q_, k_) * jnp.exp(dmat - mt[..., None])
        sa = jnp.exp(a - mt)
        num = sa[..., None] * jnp.einsum('bhld,bhde->bhle', q_, C) + jnp.einsum('bhls,bhse->bhle', s, v_)
        den = sa * jnp.einsum('bhld,bhd->bhl', q_, n) + jnp.sum(s, axis=-1)
        h = num / jnp.maximum(jnp.abs(den), jnp.exp(-mt))[..., None]
        bL = b[..., -1]
        lw = bL[..., None] - b + i_
        m_new = jnp.maximum(bL + m, jnp.max(lw, axis=-1))
        dec = jnp.exp(bL + m - m_new)
        wk = jnp.exp(lw - m_new[..., None])
        C = dec[..., None, None] * C + jnp.einsum('bhl,bhld,bhle->bhde', wk, k_, v_)
        n = dec[..., None] * n + jnp.einsum('bhl,bhld->bhd', wk, k_)
        return (C, n, m_new), h

    init = (jnp.zeros((B, H, DK, DV), jnp.float32), jnp.zeros((B, H, DK), jnp.float32), jnp.zeros((B, H), jnp.float32))
    _, h = lax.scan(step, init, xs)
    return jnp.moveaxis(h, 0, 2).reshape(B, H, S, DV)


def gated_delta_chunked(q, k, v, beta, g):
    B, H, S, DK = q.shape
    DV = v.shape[-1]
    L = D_CHUNK
    N = S // L
    rs = lambda t: t.reshape(B, H, N, L, *t.shape[3:])
    q, k, v, beta, g = rs(q), rs(k), rs(v), rs(beta), rs(g)
    g = jnp.cumsum(g, axis=-1)
    tril = jnp.tril(jnp.ones((L, L), dtype=bool))
    strict = jnp.tril(jnp.ones((L, L), dtype=bool), -1)
    decay = jnp.exp(jnp.where(tril, g[..., :, None] - g[..., None, :], -jnp.inf))
    kb = k * beta[..., None]
    amat = jnp.where(strict, jnp.einsum('bhnid,bhnjd->bhnij', kb, k) * decay, 0.0)
    rhs = jnp.concatenate([v * beta[..., None], kb * jnp.exp(g)[..., None]], axis=-1)
    sol = lax.linalg.triangular_solve(amat + jnp.eye(L, dtype=amat.dtype), rhs, left_side=True, lower=True)
    u, w = sol[..., :DV], sol[..., DV:]
    qk = jnp.where(tril, jnp.einsum('bhnid,bhnjd->bhnij', q, k) * decay, 0.0)
    mv = lambda t: jnp.moveaxis(t, 2, 0)

    def step(st, inp):
        q_, k_, u_, w_, qk_, g_ = inp
        v_new = u_ - jnp.einsum('bhld,bhde->bhle', w_, st)
        o = jnp.einsum('bhld,bhde->bhle', q_ * jnp.exp(g_)[..., None], st) + jnp.einsum('bhls,bhse->bhle', qk_, v_new)
        gL = g_[..., -1]
        st = st * jnp.exp(gL)[..., None, None] + jnp.einsum('bhld,bhle->bhde', k_ * jnp.exp(gL[..., None] - g_)[..., None], v_new)
        return st, o

    _, o = lax.scan(step, jnp.zeros((B, H, DK, DV), jnp.float32), (mv(q), mv(k), mv(u), mv(w), mv(qk), mv(g)))
    return jnp.moveaxis(o, 0, 2).reshape(B, H, S, DV)


def token_mixers(h, w_in, m_gate_b, m_norm_g, dn_conv_w, dn_a_log, dn_dt_bias, dn_norm_g, cm_conv_w, w_br_m, w_br_d, w_br_c, w_out):
    Bn, S, _ = h.shape
    f32 = jnp.float32
    idx = np.cumsum(np.array(COLS))[:-1].tolist()
    m_q, m_k, m_v, m_o, m_if, d_qkv, d_z, d_ba, c_bcx, br_g = jnp.split(h @ w_in, idx, axis=-1)
    heads = lambda t, nh: t.reshape(Bn, S, nh, -1).transpose(0, 2, 1, 3)
    flip = lambda t: jnp.flip(t, axis=2)

    q = heads(m_q, M_HEADS).astype(f32) * (M_DQK ** -0.5)
    k = heads(m_k, M_HEADS).astype(f32)
    v = heads(m_v, M_HEADS).astype(f32)
    gts = (m_if.astype(f32) + m_gate_b.astype(f32)).reshape(Bn, S, 4, M_HEADS).transpose(2, 0, 3, 1)
    h_f = mlstm_chunked(q, k, v, gts[0], jax.nn.log_sigmoid(gts[1]))
    h_b = flip(mlstm_chunked(flip(q), flip(k), flip(v), flip(gts[2]), jax.nn.log_sigmoid(flip(gts[3]))))
    hm = rmsnorm(h_f + h_b, m_norm_g).transpose(0, 2, 1, 3).reshape(Bn, S, M_V).astype(h.dtype)
    out_a = (jax.nn.sigmoid(m_o) * hm) @ w_br_m

    qkv = jax.nn.silu(dwconv3(d_qkv, dn_conv_w))
    dq, dk, dv = jnp.split(qkv, [D_QK, 2 * D_QK], axis=-1)
    dq = l2norm(heads(dq, D_HEADS).astype(f32)) * (D_DK ** -0.5)
    dk = l2norm(heads(dk, D_HEADS).astype(f32))
    dv = heads(dv, D_HEADS).astype(f32)
    ba = d_ba.astype(f32).reshape(Bn, S, 4, D_HEADS).transpose(2, 0, 3, 1)
    a_log = dn_a_log.astype(f32)
    dtb = dn_dt_bias.astype(f32)
    g_f = -jnp.exp(a_log[0])[:, None] * jax.nn.softplus(ba[1] + dtb[0][:, None])
    g_b = -jnp.exp(a_log[1])[:, None] * jax.nn.softplus(ba[3] + dtb[1][:, None])
    o_f = gated_delta_chunked(dq, dk, dv, jax.nn.sigmoid(ba[0]), g_f)
    o_b = flip(gated_delta_chunked(flip(dq), flip(dk), flip(dv), flip(jax.nn.sigmoid(ba[2])), flip(g_b)))
    od = rmsnorm(o_f + o_b, dn_norm_g).transpose(0, 2, 1, 3).reshape(Bn, S, D_V).astype(h.dtype)
    out_b = (od * jax.nn.silu(d_z)) @ w_br_d

    cb, cc, cx = jnp.split(c_bcx, 3, axis=-1)
    out_c = (cb * dwconv3(cc * cx, cm_conv_w)) @ w_br_c

    ga, gb, gc = jnp.split(br_g, 3, axis=-1)
    merged = jax.nn.sigmoid(ga) * out_a + jax.nn.sigmoid(gb) * out_b + jax.nn.sigmoid(gc) * out_c
    return merged @ w_out


def conv_ffn(h, w_up, conv_w, w_down):
    a, b = jnp.split(h @ w_up, 2, axis=-1)
    return (jax.nn.silu(dwconv3(a, conv_w)) * b) @ w_down


def trunk(x, c, norm1_g, w_ada, b_ada, w_in, m_gate_b, m_norm_g, dn_conv_w, dn_a_log, dn_dt_bias, dn_norm_g, cm_conv_w, w_br_m, w_br_d, w_br_c, w_out, norm2_g, w_up, ffn_conv_w, w_down, final_g):
    for l in range(DEPTH):
        mod = jax.nn.silu(c) @ w_ada[l] + b_ada[l]
        sh1, sc1, g1, sh2, sc2, g2 = [t[:, None, :] for t in jnp.split(mod, 6, axis=-1)]
        h = rmsnorm(x, norm1_g[l]) * (1 + sc1) + sh1
        x = x + g1 * token_mixers(h, w_in[l], m_gate_b[l], m_norm_g[l], dn_conv_w[l], dn_a_log[l], dn_dt_bias[l], dn_norm_g[l], cm_conv_w[l], w_br_m[l], w_br_d[l], w_br_c[l], w_out[l])
        h = rmsnorm(x, norm2_g[l]) * (1 + sc2) + sh2
        x = x + g2 * conv_ffn(h, w_up[l], ffn_conv_w[l], w_down[l])
    return rmsnorm(x, final_g)


def setup_inputs(seed: int = 0) -> dict:
    key = jax.random.key(seed)
    ks = jax.random.split(key, 32)
    nrm = lambda k, shape, s: jax.random.normal(k, shape, jnp.float32) * s
    D = D_MODEL
    ib = nrm(ks[8], (DEPTH, 2, M_HEADS), 0.1)
    fb = jnp.linspace(3.0, 6.0, M_HEADS)[None, None, :] + nrm(ks[9], (DEPTH, 2, M_HEADS), 0.1)
    m_gate_b = jnp.stack([ib, fb], axis=2).reshape(DEPTH, 4 * M_HEADS)
    dt = jnp.exp(jax.random.uniform(ks[12], (DEPTH, 2, D_HEADS), jnp.float32, np.log(1e-3), np.log(1e-1)))
    dn_dt_bias = dt + jnp.log(-jnp.expm1(-dt))
    dn_a_log = jnp.log(jax.random.uniform(ks[13], (DEPTH, 2, D_HEADS), jnp.float32, 1.0, 16.0))
    return {
        'x_prompt': nrm(ks[0], (BATCH, SEQ, D), 1.0),
        'x_sample': nrm(ks[1], (DEC_BATCH, DEC_SEQ, D), 1.0),
        'c_prompt': nrm(ks[2], (BATCH, D), 1.0),
        'c_sample': nrm(ks[3], (DEC_BATCH, D), 1.0),
        'norm1_g': 1.0 + nrm(ks[4], (DEPTH, D), 0.02),
        'w_ada': nrm(ks[5], (DEPTH, D, 6 * D), 0.5 * D ** -0.5),
        'b_ada': nrm(ks[6], (DEPTH, 6 * D), 0.01),
        'w_in': nrm(ks[7], (DEPTH, D, PROJ_WIDTH), D ** -0.5),
        'm_gate_b': m_gate_b,
        'm_norm_g': 1.0 + nrm(ks[10], (DEPTH, M_DV), 0.02),
        'dn_conv_w': nrm(ks[11], (DEPTH, CONV_W, 2 * D_QK + D_V), CONV_W ** -0.5),
        'dn_a_log': dn_a_log,
        'dn_dt_bias': dn_dt_bias,
        'dn_norm_g': 1.0 + nrm(ks[14], (DEPTH, D_DV), 0.02),
        'cm_conv_w': nrm(ks[15], (DEPTH, CONV_W, C_WIDTH), CONV_W ** -0.5),
        'w_br_m': nrm(ks[16], (DEPTH, M_V, D), M_V ** -0.5),
        'w_br_d': nrm(ks[17], (DEPTH, D_V, D), D_V ** -0.5),
        'w_br_c': nrm(ks[18], (DEPTH, C_WIDTH, D), C_WIDTH ** -0.5),
        'w_out': nrm(ks[19], (DEPTH, D, D), D ** -0.5),
        'norm2_g': 1.0 + nrm(ks[20], (DEPTH, D), 0.02),
        'w_up': nrm(ks[21], (DEPTH, D, 2 * D_FF), D ** -0.5),
        'ffn_conv_w': nrm(ks[22], (DEPTH, CONV_W, D_FF), CONV_W ** -0.5),
        'w_down': nrm(ks[23], (DEPTH, D_FF, D), D_FF ** -0.5),
        'final_g': 1.0 + nrm(ks[24], (D,), 0.02),
    }


def reference(x_prompt, x_sample, c_prompt, c_sample, norm1_g, w_ada, b_ada, w_in, m_gate_b, m_norm_g, dn_conv_w, dn_a_log, dn_dt_bias, dn_norm_g, cm_conv_w, w_br_m, w_br_d, w_br_c, w_out, norm2_g, w_up, ffn_conv_w, w_down, final_g):
    y_prompt = trunk(x_prompt, c_prompt, norm1_g, w_ada, b_ada, w_in, m_gate_b, m_norm_g, dn_conv_w, dn_a_log, dn_dt_bias, dn_norm_g, cm_conv_w, w_br_m, w_br_d, w_br_c, w_out, norm2_g, w_up, ffn_conv_w, w_down, final_g)
    y_sample = trunk(x_sample, c_sample, norm1_g, w_ada, b_ada, w_in, m_gate_b, m_norm_g, dn_conv_w, dn_a_log, dn_dt_bias, dn_norm_g, cm_conv_w, w_br_m, w_br_d, w_br_c, w_out, norm2_g, w_up, ffn_conv_w, w_down, final_g)
    return (y_prompt, y_sample)
```

```python
import functools

import jax
import jax.numpy as jnp
from jax import lax
from jax.experimental import pallas as pl
from jax.experimental.pallas import tpu as pltpu

F32 = jnp.float32
BF16 = jnp.bfloat16

D_MODEL = 1024
DEPTH = 4
M_HEADS = 4
M_DQK = 128
M_DV = 256
M_CHUNK = 128
D_HEADS = 8
D_DK = 128
D_DV = 128
G_CHUNK = 128
D_FF = 2816
EPS = 1e-6

M_QK = M_HEADS * M_DQK
M_V = M_HEADS * M_DV
D_QK = D_HEADS * D_DK
D_V = D_HEADS * D_DV
COLS = (M_QK, M_QK, M_V, M_V, 4 * M_HEADS, 2 * D_QK + D_V, D_V, 4 * D_HEADS, 3 * D_MODEL, 3 * D_MODEL)

LANE = 128
P_MAIN = 13 * 1024
BLK_MO, BLK_DZ, BLK_CB, BLK_CC, BLK_CX, BLK_GA, BLK_GB, BLK_GC = 2, 6, 7, 8, 9, 10, 11, 12
GATE_M = 0
GATE_D = 4 * M_HEADS

P_DTYPE = jnp.float32
VMEM_LIMIT = 56 * 1024 * 1024


def _cparams(sem):
    return pltpu.CompilerParams(dimension_semantics=sem, vmem_limit_bytes=VMEM_LIMIT)


def _silu(x):
    return x * (1.0 / (1.0 + jnp.exp(-x)))


def _sigmoid(x):
    return 1.0 / (1.0 + jnp.exp(-x))


def _softplus(x):
    return jnp.maximum(x, 0.0) + jnp.log(1.0 + jnp.exp(-jnp.abs(x)))


def _dot(a, b):
    return jnp.dot(a, b, preferred_element_type=F32)


def _dot_nt(a, b):
    return lax.dot_general(a, b, (((1,), (1,)), ((), ())), preferred_element_type=F32)


def _dot_tn(a, b):
    return lax.dot_general(a, b, (((0,), (0,)), ((), ())), preferred_element_type=F32)


def _dot_hi(a, b):
    return jnp.dot(a, b, preferred_element_type=F32, precision=lax.Precision.HIGHEST)


def _scan_masks(n, reverse):
    r = lax.broadcasted_iota(jnp.int32, (n, n), 0)
    c = lax.broadcasted_iota(jnp.int32, (n, n), 1)
    if reverse:
        return c >= r, c > r, r >= c
    return c <= r, c < r, r <= c


def _mod_kernel(c_ref, w_ref, b_ref, o_ref):
    o_ref[...] = _dot_hi(_silu(c_ref[...]), w_ref[...]) + b_ref[...]


def _modulation(c, w_ada, b_ada):
    bn = c.shape[0]
    out = pl.pallas_call(
        _mod_kernel,
        grid=(DEPTH, 6),
        in_specs=[
            pl.BlockSpec((bn, D_MODEL), lambda l, j: (0, 0)),
            pl.BlockSpec((None, D_MODEL, D_MODEL), lambda l, j: (l, 0, j)),
            pl.BlockSpec((None, 1, D_MODEL), lambda l, j: (l, 0, j)),
        ],
        out_specs=pl.BlockSpec((None, bn, D_MODEL), lambda l, j: (l, 0, j)),
        out_shape=jax.ShapeDtypeStruct((DEPTH, bn, 6 * D_MODEL), F32),
        compiler_params=_cparams(("arbitrary", "arbitrary")),
        name="adaln_mod",
    )(c, w_ada, b_ada.reshape(DEPTH, 1, 6 * D_MODEL))
    return out.reshape(DEPTH, bn, 6, D_MODEL)


def _norm_mod(x, g, scale, shift):
    y = x * lax.rsqrt(jnp.mean(x * x, axis=-1, keepdims=True) + EPS)
    return (y * g) * (1.0 + scale) + shift


def _proj_kernel(x_ref, mod_ref, g_ref, w_ref, wg_ref, wgt_ref, o_ref, og_ref, ogt_ref, h_ref):
    @pl.when(pl.program_id(2) == 0)
    def _():
        h = _norm_mod(x_ref[...], g_ref[...], mod_ref[1:2, :], mod_ref[0:1, :]).astype(BF16)
        h_ref[...] = h
        og_ref[...] = _dot(h, wg_ref[...])
        ogt_ref[...] = _dot_nt(wgt_ref[...], h)

    o_ref[...] = _dot(h_ref[...], w_ref[...]).astype(o_ref.dtype)


def _proj_in(x, mod, norm_g, w_main, w_gate, w_gate_t):
    bn, s, _ = x.shape
    tm = min(s, 1024)
    tn = 1664
    return pl.pallas_call(
        _proj_kernel,
        grid=(bn, s // tm, P_MAIN // tn),
        in_specs=[
            pl.BlockSpec((None, tm, D_MODEL), lambda b, i, j: (b, i, 0)),
            pl.BlockSpec((None, 6, D_MODEL), lambda b, i, j: (b, 0, 0)),
            pl.BlockSpec((1, D_MODEL), lambda b, i, j: (0, 0)),
            pl.BlockSpec((D_MODEL, tn), lambda b, i, j: (0, j)),
            pl.BlockSpec((D_MODEL, LANE), lambda b, i, j: (0, 0)),
            pl.BlockSpec((LANE, D_MODEL), lambda b, i, j: (0, 0)),
        ],
        out_specs=[
            pl.BlockSpec((None, tm, tn), lambda b, i, j: (b, i, j)),
            pl.BlockSpec((None, tm, LANE), lambda b, i, j: (b, i, 0)),
            pl.BlockSpec((None, LANE, tm), lambda b, i, j: (b, 0, i)),
        ],
        out_shape=[
            jax.ShapeDtypeStruct((bn, s, P_MAIN), P_DTYPE),
            jax.ShapeDtypeStruct((bn, s, LANE), F32),
            jax.ShapeDtypeStruct((bn, LANE, s), F32),
        ],
        scratch_shapes=[pltpu.VMEM((tm, D_MODEL), BF16)],
        compiler_params=_cparams(("arbitrary", "arbitrary", "arbitrary")),
        name="proj_in",
    )(x, mod, norm_g, w_main, w_gate, w_gate_t)


def _log_sigmoid(x):
    return jnp.minimum(x, 0.0) - jnp.log(1.0 + jnp.exp(-jnp.abs(x)))


def _mlstm_kernel(qf_ref, kf_ref, vf_ref, gf_ref, gtf_ref, qb_ref, kb_ref, vb_ref, gb_ref, gtb_ref,
                  bias_r_ref, bias_c_ref, hf_ref, hb_ref, c_ref, n_ref, m_ref):
    @pl.when(pl.program_id(1) == 0)
    def _():
        c_ref[...] = jnp.zeros_like(c_ref)
        n_ref[...] = jnp.zeros_like(n_ref)
        m_ref[...] = jnp.zeros_like(m_ref)

    L = M_CHUNK
    dirs = (
        (qf_ref, kf_ref, vf_ref, gf_ref, gtf_ref, hf_ref, False, L - 1, 0),
        (qb_ref, kb_ref, vb_ref, gb_ref, gtb_ref, hb_ref, True, 0, 2),
    )
    for d, (q_ref, k_ref, v_ref, g_ref, gt_ref, h_ref, reverse, last, gbase) in enumerate(dirs):
        mask, _, mask_t = _scan_masks(L, reverse)
        gc = g_ref[...] + bias_r_ref[...]
        gr = gt_ref[...] + bias_c_ref[...]
        bc_all = _dot_hi(mask.astype(F32), _log_sigmoid(gc))
        br_all = _dot_hi(_log_sigmoid(gr), mask_t.astype(F32))
        for h in range(M_HEADS):
            ci = GATE_M + gbase * M_HEADS + h
            cf = ci + M_HEADS
            idx = d * M_HEADS + h
            b_col = bc_all[:, cf:cf + 1]
            b_row = br_all[cf:cf + 1, :]
            i_col = gc[:, ci:ci + 1]
            i_row = gr[ci:ci + 1, :]
            m_old = m_ref[idx:idx + 1, 0:1]
            n_old = n_ref[idx:idx + 1, :]
            c_old = c_ref[idx]
            q = q_ref[:, h * M_DQK:(h + 1) * M_DQK].astype(F32)
            k = k_ref[:, h * M_DQK:(h + 1) * M_DQK].astype(F32)
            vb16 = v_ref[:, h * M_DV:(h + 1) * M_DV].astype(BF16)
            qb16 = q.astype(BF16)

            a_col = b_col + m_old
            dm = jnp.where(mask, b_col - b_row + i_row, -jnp.inf)
            mt = jnp.maximum(a_col, jnp.max(dm, axis=-1, keepdims=True))
            s = _dot_nt(qb16, k.astype(BF16)) * jnp.exp(dm - mt)
            sa = jnp.exp(a_col - mt)
            num = sa * _dot(qb16, c_old.astype(BF16)) + _dot(s.astype(BF16), vb16)
            den = sa * jnp.sum(q * n_old, axis=-1, keepdims=True) + jnp.sum(s, axis=-1, keepdims=True)
            inv = 1.0 / jnp.maximum(jnp.abs(den), jnp.exp(-mt))
            h_ref[:, h * M_DV:(h + 1) * M_DV] = num * inv

            b_tot = b_col[last:last + 1, :]
            lw_col = b_tot - b_col + i_col
            lw_row = b_tot - b_row + i_row
            m_new = jnp.maximum(b_tot + m_old, jnp.max(lw_row, axis=-1, keepdims=True))
            dec = jnp.exp(b_tot + m_old - m_new)
            wkk = jnp.exp(lw_col - m_new) * k
            c_ref[idx] = dec * c_old + _dot_tn(wkk.astype(BF16), vb16)
            n_ref[idx:idx + 1, :] = dec * n_old + jnp.sum(wkk, axis=0, keepdims=True)
            m_ref[idx:idx + 1, :] = jnp.broadcast_to(m_new, (1, LANE))


def _mlstm(p, g, gt, bias_r, bias_c):
    bn, s, _ = p.shape
    L = M_CHUNK
    n = s // L
    blk = lambda w, j, rev: pl.BlockSpec((None, L, w), (lambda b, i: (b, n - 1 - i, j)) if rev else (lambda b, i: (b, i, j)))
    gtblk = lambda rev: pl.BlockSpec((None, LANE, L), (lambda b, i: (b, 0, n - 1 - i)) if rev else (lambda b, i: (b, 0, i)))
    in_specs = []
    for rev in (False, True):
        in_specs += [blk(M_QK, 0, rev), blk(M_QK, 1, rev), blk(M_V, 1, rev), blk(LANE, 0, rev), gtblk(rev)]
    in_specs += [pl.BlockSpec((1, LANE), lambda b, i: (0, 0)), pl.BlockSpec((LANE, 1), lambda b, i: (0, 0))]
    return pl.pallas_call(
        _mlstm_kernel,
        grid=(bn, n),
        in_specs=in_specs,
        out_specs=[blk(M_V, 0, False), blk(M_V, 0, True)],
        out_shape=[jax.ShapeDtypeStruct((bn, s, M_V), F32)] * 2,
        scratch_shapes=[
            pltpu.VMEM((2 * M_HEADS, M_DQK, M_DV), F32),
            pltpu.VMEM((2 * M_HEADS, M_DQK), F32),
            pltpu.VMEM((2 * M_HEADS, LANE), F32),
        ],
        compiler_params=_cparams(("arbitrary", "arbitrary")),
        name="mlstm",
    )(p, p, p, g, gt, p, p, p, g, gt, bias_r, bias_c)


INV_BASE = 8


def _unit_tri_inverse(a):
    n = a.shape[0]
    r = lax.broadcasted_iota(jnp.int32, (n, n), 0)
    c = lax.broadcasted_iota(jnp.int32, (n, n), 1)
    same_block = lambda size: (r // size) == (c // size)
    a0 = jnp.where(same_block(INV_BASE), a, 0.0)
    a0b = a0.astype(BF16)
    a2 = _dot(a0b, a0b)
    a2b = a2.astype(BF16)
    a34 = _dot(a2b, jnp.concatenate([a0b, a2b], axis=1))
    p = (r == c).astype(F32) - a0 + a2 - a34[:, :n]
    t = p + _dot(p.astype(BF16), a34[:, n:].astype(BF16))
    size = INV_BASE
    while size < n:
        off = jnp.where(jnp.logical_and(same_block(2 * size), jnp.logical_not(same_block(size))), a, 0.0)
        tb = t.astype(BF16)
        t = t - _dot(tb, _dot(off.astype(BF16), tb).astype(BF16))
        size *= 2
    return t


def _gdn_prep_kernel(x_ref, xp_ref, xn_ref, g_ref, gt_ref, cw_ref, ar_ref, ac_ref, dr_ref, dc_ref,
                     v_ref, kgf_ref, qgf_ref, kdf_ref, tf_ref, qkf_ref, egf_ref,
                     kgb_ref, qgb_ref, kdb_ref, tb_ref, qkb_ref, egb_ref):
    i = pl.program_id(1)
    nblk = pl.num_programs(1)
    L = G_CHUNK
    x = x_ref[...].astype(F32)
    row = lax.broadcasted_iota(jnp.int32, x.shape, 0)
    prev_row = jnp.where(i > 0, xp_ref[7:8, :].astype(F32), 0.0)
    next_row = jnp.where(i < nblk - 1, xn_ref[0:1, :].astype(F32), 0.0)
    xm = jnp.where(row == 0, prev_row, pltpu.roll(x, 1, 0))
    xq = jnp.where(row == L - 1, next_row, pltpu.roll(x, L - 1, 0))
    y = _silu(xm * cw_ref[0:1, :] + x * cw_ref[1:2, :] + xq * cw_ref[2:3, :])
    v_ref[...] = y[:, 2 * D_QK:].astype(v_ref.dtype)

    gc = g_ref[...]
    gr = gt_ref[...]
    beta_c = _sigmoid(gc)
    beta_r = _sigmoid(gr)
    lg_c = ar_ref[...] * _softplus(gc + dr_ref[...])
    lg_r = ac_ref[...] * _softplus(gr + dc_ref[...])
    dirs = []
    cums = []
    for reverse, last, gbase, refs in (
        (False, L - 1, 0, (kgf_ref, qgf_ref, kdf_ref, tf_ref, qkf_ref, egf_ref)),
        (True, 0, 2, (kgb_ref, qgb_ref, kdb_ref, tb_ref, qkb_ref, egb_ref)),
    ):
        mask, strict, mask_t = _scan_masks(L, reverse)
        dirs.append((mask, strict, last, gbase) + refs)
        cums.append((_dot_hi(mask.astype(F32), lg_c), _dot_hi(lg_r, mask_t.astype(F32))))

    for h in range(D_HEADS):
        qh = y[:, h * D_DK:(h + 1) * D_DK]
        kh = y[:, D_QK + h * D_DK:D_QK + (h + 1) * D_DK]
        qh = qh * lax.rsqrt(jnp.sum(qh * qh, axis=-1, keepdims=True) + EPS) * (D_DK ** -0.5)
        kh = kh * lax.rsqrt(jnp.sum(kh * kh, axis=-1, keepdims=True) + EPS)
        khb = kh.astype(BF16)
        kk = _dot_nt(khb, khb)
        qk = _dot_nt(qh.astype(BF16), khb)
        for d, (mask, strict, last, gbase, kg_ref, qg_ref, kd_ref, t_ref, qkd_ref, eg_ref) in enumerate(dirs):
            cb = GATE_D + gbase * D_HEADS + h
            ca = cb + D_HEADS
            gcum_c = cums[d][0][:, ca:ca + 1]
            gcum_r = cums[d][1][ca:ca + 1, :]
            b_c = beta_c[:, cb:cb + 1]
            b_r = beta_r[cb:cb + 1, :]
            decay = jnp.exp(jnp.where(mask, gcum_c - gcum_r, -jnp.inf))
            amat = jnp.where(strict, (b_c * kk) * decay, 0.0)
            tmat = _unit_tri_inverse(amat)
            sl = slice(h * L, (h + 1) * L)
            t_ref[:, sl] = (tmat * b_r).astype(t_ref.dtype)
            qkd_ref[:, sl] = (qk * decay).astype(qkd_ref.dtype)
            g_tot = gcum_c[last:last + 1, :]
            eg = jnp.exp(gcum_c)
            hs = slice(h * D_DK, (h + 1) * D_DK)
            kg_ref[:, hs] = (kh * eg).astype(kg_ref.dtype)
            qg_ref[:, hs] = (qh * eg).astype(qg_ref.dtype)
            kd_ref[:, hs] = (kh * jnp.exp(g_tot - gcum_c)).astype(kd_ref.dtype)
            eg_ref[h:h + 1, :] = jnp.broadcast_to(jnp.exp(g_tot), (1, LANE))


def _gdn_prep(p, g, gt, conv_w, a_row, a_col, dt_row, dt_col):
    bn, s, _ = p.shape
    L = G_CHUNK
    n = s // L
    w3 = 2 * D_QK + D_V
    r8 = L // 8
    tok = lambda w: pl.BlockSpec((None, L, w), lambda b, i: (b, i, 0))
    const = lambda shape: pl.BlockSpec(shape, lambda b, i: (0,) * len(shape))
    big = jax.ShapeDtypeStruct((bn, s, D_QK), BF16)
    egs = jax.ShapeDtypeStruct((bn, n * D_HEADS, LANE), F32)
    per_dir_specs = [tok(D_QK)] * 5 + [pl.BlockSpec((None, D_HEADS, LANE), lambda b, i: (b, i, 0))]
    per_dir_shapes = [big] * 5 + [egs]
    return pl.pallas_call(
        _gdn_prep_kernel,
        grid=(bn, n),
        in_specs=[
            pl.BlockSpec((None, L, w3), lambda b, i: (b, i, 1)),
            pl.BlockSpec((None, 8, w3), lambda b, i: (b, jnp.maximum(i * r8 - 1, 0), 1)),
            pl.BlockSpec((None, 8, w3), lambda b, i: (b, jnp.minimum((i + 1) * r8, s // 8 - 1), 1)),
            pl.BlockSpec((None, L, LANE), lambda b, i: (b, i, 0)),
            pl.BlockSpec((None, LANE, L), lambda b, i: (b, 0, i)),
            const((3, w3)), const((1, LANE)), const((LANE, 1)), const((1, LANE)), const((LANE, 1)),
        ],
        out_specs=[tok(D_V)] + per_dir_specs * 2,
        out_shape=[big] + per_dir_shapes * 2,
        compiler_params=_cparams(("arbitrary", "arbitrary")),
        name="gdn_prep",
    )(p, p, p, g, gt, conv_w, a_row, a_col, dt_row, dt_col)


def _gdn_scan_kernel(vf_ref, kgf_ref, qgf_ref, kdf_ref, tf_ref, qkf_ref, egf_ref,
                     vb_ref, kgb_ref, qgb_ref, kdb_ref, tb_ref, qkb_ref, egb_ref,
                     of_ref, ob_ref, s_ref):
    @pl.when(pl.program_id(1) == 0)
    def _():
        s_ref[...] = jnp.zeros_like(s_ref)

    L = G_CHUNK
    dirs = (
        (vf_ref, kgf_ref, qgf_ref, kdf_ref, tf_ref, qkf_ref, egf_ref, of_ref),
        (vb_ref, kgb_ref, qgb_ref, kdb_ref, tb_ref, qkb_ref, egb_ref, ob_ref),
    )
    for d, (v_ref, kg_ref, qg_ref, kd_ref, t_ref, qkd_ref, eg_ref, o_ref) in enumerate(dirs):
        for h in range(D_HEADS):
            idx = d * D_HEADS + h
            hs = slice(h * D_DK, (h + 1) * D_DK)
            sl = slice(h * L, (h + 1) * L)
            st = s_ref[idx]
            x = _dot(jnp.concatenate([kg_ref[:, hs], qg_ref[:, hs]], axis=0), st.astype(BF16))
            rhs = (v_ref[:, hs].astype(F32) - x[:L]).astype(BF16)
            vnew = _dot(t_ref[:, sl], rhs).astype(BF16)
            o_ref[:, hs] = x[L:] + _dot(qkd_ref[:, sl], vnew)
            s_ref[idx] = eg_ref[h:h + 1, :] * st + _dot_tn(kd_ref[:, hs], vnew)


def _gdn_scan(prep):
    v = prep[0]
    bn, s, _ = v.shape
    L = G_CHUNK
    n = s // L

    def specs(rev):
        im = (lambda b, i: (b, n - 1 - i, 0)) if rev else (lambda b, i: (b, i, 0))
        return [pl.BlockSpec((None, L, D_QK), im)] * 6 + [pl.BlockSpec((None, D_HEADS, LANE), im)]

    fwd_ops = [v] + list(prep[1:7])
    bwd_ops = [v] + list(prep[7:13])
    return pl.pallas_call(
        _gdn_scan_kernel,
        grid=(bn, n),
        in_specs=specs(False) + specs(True),
        out_specs=[pl.BlockSpec((None, L, D_V), lambda b, i: (b, i, 0)),
                   pl.BlockSpec((None, L, D_V), lambda b, i: (b, n - 1 - i, 0))],
        out_shape=[jax.ShapeDtypeStruct((bn, s, D_V), F32)] * 2,
        scratch_shapes=[pltpu.VMEM((2 * D_HEADS, D_DK, D_DV), F32)],
        compiler_params=_cparams(("arbitrary", "arbitrary")),
        name="gdn_scan",
    )(*fwd_ops, *bwd_ops)


def _head_rmsnorm(x, g, width):
    outs = []
    for h in range(x.shape[1] // width):
        xs = x[:, h * width:(h + 1) * width]
        outs.append(xs * lax.rsqrt(jnp.mean(xs * xs, axis=-1, keepdims=True) + EPS))
    return jnp.concatenate(outs, axis=1) * g


def _merge_kernel(x_ref, hf_ref, hb_ref, of_ref, ob_ref, mo_ref, dz_ref, cb_ref, cc_ref, cx_ref,
                  ga_ref, gb_ref, gc_ref, ccp_ref, cxp_ref, ccn_ref, cxn_ref, mod_ref,
                  mg_ref, dg_ref, cw_ref, wm_ref, wd_ref, wc_ref, wo_ref, o_ref):
    i = pl.program_id(1)
    nblk = pl.num_programs(1)
    f = lambda ref: ref[...].astype(F32)
    hm = _head_rmsnorm(hf_ref[...] + hb_ref[...], mg_ref[...], M_DV)
    out_a = _dot((_sigmoid(f(mo_ref)) * hm).astype(BF16), wm_ref[...])
    od = _head_rmsnorm(of_ref[...] + ob_ref[...], dg_ref[...], D_DV)
    out_b = _dot((od * _silu(f(dz_ref))).astype(BF16), wd_ref[...])

    u = f(cc_ref) * f(cx_ref)
    tm = u.shape[0]
    row = lax.broadcasted_iota(jnp.int32, u.shape, 0)
    prev_row = jnp.where(i > 0, f(ccp_ref)[7:8, :] * f(cxp_ref)[7:8, :], 0.0)
    next_row = jnp.where(i < nblk - 1, f(ccn_ref)[0:1, :] * f(cxn_ref)[0:1, :], 0.0)
    um = jnp.where(row == 0, prev_row, pltpu.roll(u, 1, 0))
    up = jnp.where(row == tm - 1, next_row, pltpu.roll(u, tm - 1, 0))
    conv = um * cw_ref[0:1, :] + u * cw_ref[1:2, :] + up * cw_ref[2:3, :]
    out_c = _dot((f(cb_ref) * conv).astype(BF16), wc_ref[...])

    merged = _sigmoid(f(ga_ref)) * out_a + _sigmoid(f(gb_ref)) * out_b + _sigmoid(f(gc_ref)) * out_c
    o_ref[...] = x_ref[...] + mod_ref[2:3, :] * _dot(merged.astype(BF16), wo_ref[...])


def _merge(x, hf, hb, of, ob, p, mod, m_norm_g, dn_norm_g, cm_conv_w, w_br_m, w_br_d, w_br_c, w_out):
    bn, s, _ = x.shape
    tm = min(s, 256)
    r8 = tm // 8
    tok = pl.BlockSpec((None, tm, D_MODEL), lambda b, i: (b, i, 0))
    pblk = lambda j: pl.BlockSpec((None, tm, D_MODEL), lambda b, i: (b, i, j))
    prev = lambda j: pl.BlockSpec((None, 8, D_MODEL), lambda b, i: (b, jnp.maximum(i * r8 - 1, 0), j))
    nxt = lambda j: pl.BlockSpec((None, 8, D_MODEL), lambda b, i: (b, jnp.minimum((i + 1) * r8, s // 8 - 1), j))
    const = lambda shape: pl.BlockSpec(shape, lambda b, i: (0,) * len(shape))
    wspec = const((D_MODEL, D_MODEL))
    return pl.pallas_call(
        _merge_kernel,
        grid=(bn, s // tm),
        in_specs=[tok] * 5 + [pblk(j) for j in (BLK_MO, BLK_DZ, BLK_CB, BLK_CC, BLK_CX, BLK_GA, BLK_GB, BLK_GC)]
        + [prev(BLK_CC), prev(BLK_CX), nxt(BLK_CC), nxt(BLK_CX)]
        + [pl.BlockSpec((None, 6, D_MODEL), lambda b, i: (b, 0, 0))]
        + [const((1, D_MODEL)), const((1, D_MODEL)), const((3, D_MODEL))] + [wspec] * 4,
        out_specs=tok,
        out_shape=jax.ShapeDtypeStruct((bn, s, D_MODEL), F32),
        compiler_params=_cparams(("arbitrary", "arbitrary")),
        name="merge",
    )(x, hf, hb, of, ob, p, p, p, p, p, p, p, p, p, p, p, p, mod, m_norm_g, dn_norm_g, cm_conv_w,
      w_br_m, w_br_d, w_br_c, w_out)


FF_TILE = 256


def _ffn_kernel(x_ref, xp_ref, xn_ref, mod_ref, g_ref, wu_ref, cw_ref, wd_ref, fg_ref, o_ref, acc_ref, *, final):
    i = pl.program_id(1)
    nblk = pl.num_programs(1)
    x = x_ref[...]
    tm = x.shape[0]
    xa = jnp.concatenate([xp_ref[...], x, xn_ref[...]], axis=0)
    hf = _norm_mod(xa, g_ref[...], mod_ref[4:5, :], mod_ref[3:4, :])
    h = hf.astype(BF16)
    hmid = hf[8:8 + tm].astype(BF16)
    rows = tm + 16
    row = lax.broadcasted_iota(jnp.int32, (rows, FF_TILE), 0)
    kill_prev = jnp.logical_and(row == 7, i == 0)
    kill_next = jnp.logical_and(row == tm + 8, i == nblk - 1)
    kill = jnp.logical_or(kill_prev, kill_next)
    for c in range(D_FF // FF_TILE):
        cs = slice(c * FF_TILE, (c + 1) * FF_TILE)
        a = _dot(h, wu_ref[:, cs])
        a = jnp.where(kill, 0.0, a)
        b = _dot(hmid, wu_ref[:, D_FF + c * FF_TILE:D_FF + (c + 1) * FF_TILE])
        conv = (pltpu.roll(a, 1, 0) * cw_ref[0:1, cs] + a * cw_ref[1:2, cs]
                + pltpu.roll(a, rows - 1, 0) * cw_ref[2:3, cs])[8:8 + tm]
        y = (_silu(conv) * b).astype(BF16)
        contrib = _dot(y, wd_ref[cs, :])
        if c == 0:
            acc_ref[...] = contrib
        else:
            acc_ref[...] += contrib
    out = x + mod_ref[5:6, :] * acc_ref[...]
    if final:
        out = (out * lax.rsqrt(jnp.mean(out * out, axis=-1, keepdims=True) + EPS)) * fg_ref[...]
    o_ref[...] = out


def _ffn(x, mod, norm_g, w_up, conv_w, w_down, final_g, final):
    bn, s, _ = x.shape
    tm = min(s, 512)
    r8 = tm // 8
    const = lambda shape: pl.BlockSpec(shape, lambda b, i: (0,) * len(shape))
    return pl.pallas_call(
        functools.partial(_ffn_kernel, final=final),
        grid=(bn, s // tm),
        in_specs=[
            pl.BlockSpec((None, tm, D_MODEL), lambda b, i: (b, i, 0)),
            pl.BlockSpec((None, 8, D_MODEL), lambda b, i: (b, jnp.maximum(i * r8 - 1, 0), 0)),
            pl.BlockSpec((None, 8, D_MODEL), lambda b, i: (b, jnp.minimum((i + 1) * r8, s // 8 - 1), 0)),
            pl.BlockSpec((None, 6, D_MODEL), lambda b, i: (b, 0, 0)),
            const((1, D_MODEL)), const((D_MODEL, 2 * D_FF)), const((3, D_FF)), const((D_FF, D_MODEL)),
            const((1, D_MODEL)),
        ],
        out_specs=pl.BlockSpec((None, tm, D_MODEL), lambda b, i: (b, i, 0)),
        out_shape=jax.ShapeDtypeStruct((bn, s, D_MODEL), F32),
        scratch_shapes=[pltpu.VMEM((tm, D_MODEL), F32)],
        compiler_params=_cparams(("arbitrary", "arbitrary")),
        name="ffn_final" if final else "ffn",
    )(x, x, x, mod, norm_g, w_up, conv_w, w_down, final_g)


def _prepare(w_in, m_gate_b, m_norm_g, dn_a_log, dn_dt_bias, dn_norm_g):
    bounds = [0]
    for cw in COLS:
        bounds.append(bounds[-1] + cw)
    seg = [w_in[:, :, bounds[j]:bounds[j + 1]] for j in range(len(COLS))]
    m_q, m_k, m_v, m_o, m_if, d_qkv, d_z, d_ba, c_bcx, br_g = seg
    w_main = jnp.concatenate([m_q * (M_DQK ** -0.5), m_k, m_v, m_o, d_qkv, d_z, c_bcx, br_g], axis=-1).astype(BF16)
    pad = jnp.zeros(w_in.shape[:2] + (LANE - 4 * M_HEADS - 4 * D_HEADS,), w_in.dtype)
    w_gate = jnp.concatenate([m_if, d_ba, pad], axis=-1).astype(BF16)
    w_gate_t = jnp.swapaxes(w_gate, 1, 2)

    lanes = lambda t, off: jnp.pad(t, ((0, 0), (off, LANE - off - t.shape[1])))
    bias = lanes(m_gate_b, GATE_M)
    neg_a = -jnp.exp(dn_a_log.astype(F32))
    zeros_h = jnp.zeros((DEPTH, D_HEADS), F32)
    a_l = lanes(jnp.concatenate([zeros_h, neg_a[:, 0], zeros_h, neg_a[:, 1]], axis=-1), GATE_D)
    dtb = dn_dt_bias.astype(F32)
    dt_l = lanes(jnp.concatenate([zeros_h, dtb[:, 0], zeros_h, dtb[:, 1]], axis=-1), GATE_D)
    as_row = lambda t: t[:, None, :]
    as_col = lambda t: t[:, :, None]
    return dict(
        w_main=w_main, w_gate=w_gate, w_gate_t=w_gate_t,
        bias_r=as_row(bias), bias_c=as_col(bias),
        a_r=as_row(a_l), a_c=as_col(a_l), dt_r=as_row(dt_l), dt_c=as_col(dt_l),
        m_norm=jnp.tile(m_norm_g, (1, M_HEADS))[:, None, :], dn_norm=jnp.tile(dn_norm_g, (1, D_HEADS))[:, None, :],
    )


def _trunk(x, c, prm, norm1_g, w_ada, b_ada, dn_conv_w, cm_conv_w, w_br_m, w_br_d, w_br_c, w_out, norm2_g,
           w_up, ffn_conv_w, w_down, final_g):
    mod = _modulation(c, w_ada, b_ada)
    fg = final_g[None, :]
    for l in range(DEPTH):
        p, g, gt = _proj_in(x, mod[l], norm1_g[l][None, :], prm["w_main"][l], prm["w_gate"][l], prm["w_gate_t"][l])
        hf, hb = _mlstm(p, g, gt, prm["bias_r"][l], prm["bias_c"][l])
        prep = _gdn_prep(p, g, gt, dn_conv_w[l], prm["a_r"][l], prm["a_c"][l], prm["dt_r"][l], prm["dt_c"][l])
        of, ob = _gdn_scan(prep)
        x = _merge(x, hf, hb, of, ob, p, mod[l], prm["m_norm"][l], prm["dn_norm"][l], cm_conv_w[l],
                   w_br_m[l], w_br_d[l], w_br_c[l], w_out[l])
        x = _ffn(x, mod[l], norm2_g[l][None, :], w_up[l], ffn_conv_w[l], w_down[l], fg, l == DEPTH - 1)
    return x


def kernel(x_prompt, x_sample, c_prompt, c_sample, norm1_g, w_ada, b_ada, w_in, m_gate_b, m_norm_g, dn_conv_w, dn_a_log, dn_dt_bias, dn_norm_g, cm_conv_w, w_br_m, w_br_d, w_br_c, w_out, norm2_g, w_up, ffn_conv_w, w_down, final_g):
    prm = _prepare(w_in, m_gate_b, m_norm_g, dn_a_log, dn_dt_bias, dn_norm_g)
    bf = lambda t: t.astype(BF16)
    args = (prm, norm1_g, w_ada, b_ada, dn_conv_w, cm_conv_w, bf(w_br_m), bf(w_br_d), bf(w_br_c), bf(w_out),
            norm2_g, bf(w_up), ffn_conv_w, bf(w_down), final_g)
    return (_trunk(x_prompt, c_prompt, *args), _trunk(x_sample, c_sample, *args))
```

```python
import functools

import jax
import jax.numpy as jnp
from jax import lax
from jax.experimental import pallas as pl
from jax.experimental.pallas import tpu as pltpu

F32 = jnp.float32
BF16 = jnp.bfloat16

D_MODEL = 1024
DEPTH = 4
M_HEADS = 4
M_DQK = 128
M_DV = 256
M_CHUNK = 128
D_HEADS = 8
D_DK = 128
D_DV = 128
G_CHUNK = 128
D_FF = 2816
EPS = 1e-6

M_QK = M_HEADS * M_DQK
M_V = M_HEADS * M_DV
D_QK = D_HEADS * D_DK
D_V = D_HEADS * D_DV
COLS = (M_QK, M_QK, M_V, M_V, 4 * M_HEADS, 2 * D_QK + D_V, D_V, 4 * D_HEADS, 3 * D_MODEL, 3 * D_MODEL)

LANE = 128
P_MAIN = 13 * 1024
BLK_MO, BLK_DZ, BLK_CB, BLK_CC, BLK_CX, BLK_GA, BLK_GB, BLK_GC = 2, 6, 7, 8, 9, 10, 11, 12
GATE_M = 0
GATE_D = 4 * M_HEADS

M_BATCH = 1
P_DTYPE = BF16
P_HALO = 16
VMEM_LIMIT = 56 * 1024 * 1024


def _cparams(sem):
    return pltpu.CompilerParams(dimension_semantics=sem, vmem_limit_bytes=VMEM_LIMIT)


def _silu(x):
    return x * (1.0 / (1.0 + jnp.exp(-x)))


def _sigmoid(x):
    return 1.0 / (1.0 + jnp.exp(-x))


def _softplus(x):
    return jnp.maximum(x, 0.0) + jnp.log(1.0 + jnp.exp(-jnp.abs(x)))


def _dot(a, b):
    return jnp.dot(a, b, preferred_element_type=F32)


def _dot_nt(a, b):
    return lax.dot_general(a, b, (((1,), (1,)), ((), ())), preferred_element_type=F32)


def _dot_tn(a, b):
    return lax.dot_general(a, b, (((0,), (0,)), ((), ())), preferred_element_type=F32)


def _dot_hi(a, b):
    return jnp.dot(a, b, preferred_element_type=F32, precision=lax.Precision.HIGHEST)


def _split3(x):
    h1 = x.astype(BF16)
    r1 = x - h1.astype(F32)
    h2 = r1.astype(BF16)
    h3 = (r1 - h2.astype(F32)).astype(BF16)
    return h1, h2, h3


def _cumsum_cols(mask01, x):
    h1, h2, h3 = _split3(x)
    return _dot(mask01, h1) + (_dot(mask01, h2) + _dot(mask01, h3))


def _cumsum_rows(x, mask01):
    h1, h2, h3 = _split3(x)
    return _dot(h1, mask01) + (_dot(h2, mask01) + _dot(h3, mask01))


def _scan_masks(n, reverse):
    r = lax.broadcasted_iota(jnp.int32, (n, n), 0)
    c = lax.broadcasted_iota(jnp.int32, (n, n), 1)
    if reverse:
        return c >= r, c > r, r >= c
    return c <= r, c < r, r <= c


def _mod_kernel(c_ref, w_ref, b_ref, o_ref):
    o_ref[...] = _dot_hi(_silu(c_ref[...]), w_ref[...]) + b_ref[...]


def _modulation(c, w_ada, b_ada):
    bn = c.shape[0]
    out = pl.pallas_call(
        _mod_kernel,
        grid=(DEPTH, 6),
        in_specs=[
            pl.BlockSpec((bn, D_MODEL), lambda l, j: (0, 0)),
            pl.BlockSpec((None, D_MODEL, D_MODEL), lambda l, j: (l, 0, j)),
            pl.BlockSpec((None, 1, D_MODEL), lambda l, j: (l, 0, j)),
        ],
        out_specs=pl.BlockSpec((None, bn, D_MODEL), lambda l, j: (l, 0, j)),
        out_shape=jax.ShapeDtypeStruct((DEPTH, bn, 6 * D_MODEL), F32),
        compiler_params=_cparams(("arbitrary", "arbitrary")),
        name="adaln_mod",
    )(c, w_ada, b_ada.reshape(DEPTH, 1, 6 * D_MODEL))
    return out.reshape(DEPTH, bn, 6, D_MODEL)


def _norm_mod(x, g, scale, shift):
    y = x * lax.rsqrt(jnp.mean(x * x, axis=-1, keepdims=True) + EPS)
    return (y * g) * (1.0 + scale) + shift


PROJ_SUB = 256


def _proj_kernel(x_ref, mod_ref, g_ref, w_ref, wg_ref, wgt_ref, o_ref, og_ref, ogt_ref, h_ref):
    @pl.when(pl.program_id(2) == 0)
    def _():
        h = _norm_mod(x_ref[...], g_ref[...], mod_ref[1:2, :], mod_ref[0:1, :]).astype(BF16)
        h_ref[...] = h
        og_ref[...] = _dot(h, wg_ref[...])
        ogt_ref[...] = _dot_nt(wgt_ref[...], h)

    h = h_ref[...]
    for c in range(o_ref.shape[1] // PROJ_SUB):
        cs = slice(c * PROJ_SUB, (c + 1) * PROJ_SUB)
        o_ref[:, cs] = _dot(h, w_ref[:, cs]).astype(o_ref.dtype)


def _proj_in(x, mod, norm_g, w_main, w_gate, w_gate_t):
    bn, s, _ = x.shape
    tm = min(s, 1024)
    tn = P_MAIN // 4
    return pl.pallas_call(
        _proj_kernel,
        grid=(bn, s // tm, P_MAIN // tn),
        in_specs=[
            pl.BlockSpec((None, tm, D_MODEL), lambda b, i, j: (b, i, 0)),
            pl.BlockSpec((None, 6, D_MODEL), lambda b, i, j: (b, 0, 0)),
            pl.BlockSpec((1, D_MODEL), lambda b, i, j: (0, 0)),
            pl.BlockSpec((D_MODEL, tn), lambda b, i, j: (0, j)),
            pl.BlockSpec((D_MODEL, LANE), lambda b, i, j: (0, 0)),
            pl.BlockSpec((LANE, D_MODEL), lambda b, i, j: (0, 0)),
        ],
        out_specs=[
            pl.BlockSpec((None, tm, tn), lambda b, i, j: (b, i, j)),
            pl.BlockSpec((None, tm, LANE), lambda b, i, j: (b, i, 0)),
            pl.BlockSpec((None, LANE, tm), lambda b, i, j: (b, 0, i)),
        ],
        out_shape=[
            jax.ShapeDtypeStruct((bn, s, P_MAIN), P_DTYPE),
            jax.ShapeDtypeStruct((bn, s, LANE), F32),
            jax.ShapeDtypeStruct((bn, LANE, s), F32),
        ],
        scratch_shapes=[pltpu.VMEM((tm, D_MODEL), BF16)],
        compiler_params=_cparams(("arbitrary", "arbitrary", "arbitrary")),
        name="proj_in",
    )(x, mod, norm_g, w_main, w_gate, w_gate_t)


def _log_sigmoid(x):
    return jnp.minimum(x, 0.0) - jnp.log(1.0 + jnp.exp(-jnp.abs(x)))


def _mlstm_kernel(qf_ref, kf_ref, vf_ref, gf_ref, gtf_ref, qb_ref, kb_ref, vb_ref, gb_ref, gtb_ref,
                  bias_r_ref, bias_c_ref, hf_ref, hb_ref, c_ref, n_ref, m_ref):
    @pl.when(pl.program_id(1) == 0)
    def _():
        c_ref[...] = jnp.zeros_like(c_ref)
        n_ref[...] = jnp.zeros_like(n_ref)
        m_ref[...] = jnp.zeros_like(m_ref)

    L = M_CHUNK
    dirs = (
        (qf_ref, kf_ref, vf_ref, gf_ref, gtf_ref, hf_ref, False, L - 1, 0),
        (qb_ref, kb_ref, vb_ref, gb_ref, gtb_ref, hb_ref, True, 0, 2),
    )
    units = []
    for bi, d in [(bi, d) for bi in range(M_BATCH) for d in range(2)]:
        q_ref, k_ref, v_ref, g_ref, gt_ref, h_ref, reverse, last, gbase = dirs[d]
        mask, _, mask_t = _scan_masks(L, reverse)
        gc = g_ref[bi] + bias_r_ref[...]
        gr = gt_ref[bi] + bias_c_ref[...]
        bc_all = _cumsum_cols(mask.astype(BF16), _log_sigmoid(gc))
        br_all = _cumsum_rows(_log_sigmoid(gr), mask_t.astype(BF16))
        for h in range(M_HEADS):
            ci = GATE_M + gbase * M_HEADS + h
            cf = ci + M_HEADS
            idx = (bi * 2 + d) * M_HEADS + h
            q = q_ref[bi, :, h * M_DQK:(h + 1) * M_DQK].astype(F32)
            k = k_ref[bi, :, h * M_DQK:(h + 1) * M_DQK].astype(F32)
            units.append(dict(
                idx=idx, mask=mask, last=last, h_ref=h_ref, bi=bi, hs=slice(h * M_DV, (h + 1) * M_DV),
                b_col=bc_all[:, cf:cf + 1], b_row=br_all[cf:cf + 1, :], i_col=gc[:, ci:ci + 1], i_row=gr[ci:ci + 1, :],
                m_old=m_ref[idx:idx + 1, 0:1], n_old=n_ref[idx:idx + 1, :], c_old=c_ref[idx],
                q=q, k=k, qb=q.astype(BF16), vb=v_ref[bi, :, h * M_DV:(h + 1) * M_DV].astype(BF16)))

    for u in units:
        u["qk"] = _dot_nt(u["qb"], u["k"].astype(BF16))
        u["qc"] = _dot(u["qb"], u["c_old"].astype(BF16))
    for u in units:
        a_col = u["b_col"] + u["m_old"]
        dm = jnp.where(u["mask"], u["b_col"] - u["b_row"] + u["i_row"], -jnp.inf)
        mt = jnp.maximum(a_col, jnp.max(dm, axis=-1, keepdims=True))
        u["mt"] = mt
        u["s"] = u["qk"] * jnp.exp(dm - mt)
        u["sa"] = jnp.exp(a_col - mt)
    for u in units:
        u["sv"] = _dot(u["s"].astype(BF16), u["vb"])
    for u in units:
        num = u["sa"] * u["qc"] + u["sv"]
        den = (u["sa"] * jnp.sum(u["q"] * u["n_old"], axis=-1, keepdims=True)
               + jnp.sum(u["s"], axis=-1, keepdims=True))
        inv = 1.0 / jnp.maximum(jnp.abs(den), jnp.exp(-u["mt"]))
        u["h_ref"][u["bi"], :, u["hs"]] = num * inv
    for u in units:
        idx, last = u["idx"], u["last"]
        b_tot = u["b_col"][last:last + 1, :]
        lw_col = b_tot - u["b_col"] + u["i_col"]
        lw_row = b_tot - u["b_row"] + u["i_row"]
        m_new = jnp.maximum(b_tot + u["m_old"], jnp.max(lw_row, axis=-1, keepdims=True))
        dec = jnp.exp(b_tot + u["m_old"] - m_new)
        wkk = jnp.exp(lw_col - m_new) * u["k"]
        c_ref[idx] = dec * u["c_old"] + _dot_tn(wkk.astype(BF16), u["vb"])
        n_ref[idx:idx + 1, :] = dec * u["n_old"] + jnp.sum(wkk, axis=0, keepdims=True)
        m_ref[idx:idx + 1, :] = jnp.broadcast_to(m_new, (1, LANE))


def _mlstm(p, g, gt, bias_r, bias_c):
    bn, s, _ = p.shape
    L = M_CHUNK
    n = s // L
    mb = M_BATCH
    blk = lambda w, j, rev: pl.BlockSpec((mb, L, w), (lambda b, i: (b, n - 1 - i, j)) if rev else (lambda b, i: (b, i, j)))
    gtblk = lambda rev: pl.BlockSpec((mb, LANE, L), (lambda b, i: (b, 0, n - 1 - i)) if rev else (lambda b, i: (b, 0, i)))
    in_specs = []
    for rev in (False, True):
        in_specs += [blk(M_QK, 0, rev), blk(M_QK, 1, rev), blk(M_V, 1, rev), blk(LANE, 0, rev), gtblk(rev)]
    in_specs += [pl.BlockSpec((1, LANE), lambda b, i: (0, 0)), pl.BlockSpec((LANE, 1), lambda b, i: (0, 0))]
    n_units = mb * 2 * M_HEADS
    return pl.pallas_call(
        _mlstm_kernel,
        grid=(bn // mb, n),
        in_specs=in_specs,
        out_specs=[blk(M_V, 0, False), blk(M_V, 0, True)],
        out_shape=[jax.ShapeDtypeStruct((bn, s, M_V), F32)] * 2,
        scratch_shapes=[
            pltpu.VMEM((n_units, M_DQK, M_DV), F32),
            pltpu.VMEM((n_units, M_DQK), F32),
            pltpu.VMEM((n_units, LANE), F32),
        ],
        compiler_params=_cparams(("arbitrary", "arbitrary")),
        name="mlstm",
    )(p, p, p, g, gt, p, p, p, g, gt, bias_r, bias_c)


INV_BASE = 8


def _lane_pair(a, b):
    return jnp.concatenate([a, b], axis=1)


def _blockdiag(pair):
    n = pair.shape[0]
    z = jnp.zeros((n, n), pair.dtype)
    return jnp.concatenate([_lane_pair(pair[:, :n], z), _lane_pair(z, pair[:, n:])], axis=0)


def _pair_dot(lhs, rhs):
    return _dot(lhs, _blockdiag(rhs))


def _unit_tri_inverse_pairs(a_list):
    n = a_list[0].shape[0]
    r = lax.broadcasted_iota(jnp.int32, (n, 2 * n), 0)
    c = lax.broadcasted_iota(jnp.int32, (n, 2 * n), 1) % n
    same_block = lambda size: (r // size) == (c // size)
    eye = (r == c).astype(F32)
    bf = lambda xs: [x.astype(BF16) for x in xs]
    a0 = [jnp.where(same_block(INV_BASE), a, 0.0) for a in a_list]
    a0b = bf(a0)
    a2 = [_pair_dot(x, x) for x in a0b]
    a2b = bf(a2)
    a3 = [_pair_dot(x2, x) for x2, x in zip(a2b, a0b)]
    a4 = [_pair_dot(x2, x2) for x2 in a2b]
    p = [eye - x + x2 - x3 for x, x2, x3 in zip(a0, a2, a3)]
    t = [pp + _pair_dot(pb, x4) for pp, pb, x4 in zip(p, bf(p), bf(a4))]
    size = INV_BASE
    while size < n:
        level = jnp.logical_and(same_block(2 * size), jnp.logical_not(same_block(size)))
        tb = bf(t)
        x = [_pair_dot(jnp.where(level, a, 0.0).astype(BF16), tt) for a, tt in zip(a_list, tb)]
        t = [tt - _pair_dot(tq, xx) for tt, tq, xx in zip(t, tb, bf(x))]
        size *= 2
    return t


def _gdn_prep_kernel(x_ref, xp_ref, xn_ref, g_ref, gt_ref, cw_ref, ar_ref, ac_ref, dr_ref, dc_ref,
                     v_ref, kgf_ref, qgf_ref, kdf_ref, tf_ref, qkf_ref, egf_ref,
                     kgb_ref, qgb_ref, kdb_ref, tb_ref, qkb_ref, egb_ref):
    i = pl.program_id(1)
    nblk = pl.num_programs(1)
    L = G_CHUNK
    x = x_ref[...].astype(F32)
    row = lax.broadcasted_iota(jnp.int32, x.shape, 0)
    prev_row = jnp.where(i > 0, xp_ref[P_HALO - 1:P_HALO, :].astype(F32), 0.0)
    next_row = jnp.where(i < nblk - 1, xn_ref[0:1, :].astype(F32), 0.0)
    xm = jnp.where(row == 0, prev_row, pltpu.roll(x, 1, 0))
    xq = jnp.where(row == L - 1, next_row, pltpu.roll(x, L - 1, 0))
    y = _silu(xm * cw_ref[0:1, :] + x * cw_ref[1:2, :] + xq * cw_ref[2:3, :])
    v_ref[...] = y[:, 2 * D_QK:].astype(v_ref.dtype)

    gc = g_ref[...]
    gr = gt_ref[...]
    beta_c = _sigmoid(gc)
    beta_r = _sigmoid(gr)
    lg_c = ar_ref[...] * _softplus(gc + dr_ref[...])
    lg_r = ac_ref[...] * _softplus(gr + dc_ref[...])

    qn, kn = [], []
    for h in range(D_HEADS):
        qh = y[:, h * D_DK:(h + 1) * D_DK]
        kh = y[:, D_QK + h * D_DK:D_QK + (h + 1) * D_DK]
        qn.append(qh * lax.rsqrt(jnp.sum(qh * qh, axis=-1, keepdims=True) + EPS) * (D_DK ** -0.5))
        kn.append(kh * lax.rsqrt(jnp.sum(kh * kh, axis=-1, keepdims=True) + EPS))
    kk_p, qk_p = [], []
    for hp in range(D_HEADS // 2):
        kp = _lane_pair(kn[2 * hp], kn[2 * hp + 1]).astype(BF16)
        qp = _lane_pair(qn[2 * hp], qn[2 * hp + 1]).astype(BF16)
        kbd = _blockdiag(kp)
        kk_p.append(_dot_nt(kp, kbd))
        qk_p.append(_dot_nt(qp, kbd))

    units = []
    amats = []
    for reverse, last, gbase, refs in (
        (False, L - 1, 0, (kgf_ref, qgf_ref, kdf_ref, tf_ref, qkf_ref, egf_ref)),
        (True, 0, 2, (kgb_ref, qgb_ref, kdb_ref, tb_ref, qkb_ref, egb_ref)),
    ):
        mask, strict, mask_t = _scan_masks(L, reverse)
        cum_c = _cumsum_cols(mask.astype(BF16), lg_c)
        cum_r = _cumsum_rows(lg_r, mask_t.astype(BF16))
        kg_ref, qg_ref, kd_ref, t_ref, qkd_ref, eg_ref = refs
        for hp in range(D_HEADS // 2):
            decays, a_halves, b_rows = [], [], []
            for half, h in enumerate((2 * hp, 2 * hp + 1)):
                cb = GATE_D + gbase * D_HEADS + h
                ca = cb + D_HEADS
                gcum_c = cum_c[:, ca:ca + 1]
                decay = jnp.exp(jnp.where(mask, gcum_c - cum_r[ca:ca + 1, :], -jnp.inf))
                kk = kk_p[hp][:, half * L:(half + 1) * L]
                a_halves.append(jnp.where(strict, (beta_c[:, cb:cb + 1] * kk) * decay, 0.0))
                decays.append(decay)
                b_rows.append(beta_r[cb:cb + 1, :])
                g_tot = gcum_c[last:last + 1, :]
                eg = jnp.exp(gcum_c)
                hs = slice(h * D_DK, (h + 1) * D_DK)
                kg_ref[:, hs] = (kn[h] * eg).astype(kg_ref.dtype)
                qg_ref[:, hs] = (qn[h] * eg).astype(qg_ref.dtype)
                kd_ref[:, hs] = (kn[h] * jnp.exp(g_tot - gcum_c)).astype(kd_ref.dtype)
                eg_ref[h:h + 1, :] = jnp.broadcast_to(jnp.exp(g_tot), (1, LANE))
            ps = slice(hp * 2 * L, (hp + 1) * 2 * L)
            qkd_ref[:, ps] = (qk_p[hp] * _lane_pair(*decays)).astype(qkd_ref.dtype)
            amats.append(_lane_pair(*a_halves))
            units.append((t_ref, ps, _lane_pair(*b_rows)))

    for (t_ref, ps, b_row), tmat in zip(units, _unit_tri_inverse_pairs(amats)):
        t_ref[:, ps] = (tmat * b_row).astype(t_ref.dtype)


def _gdn_prep(p, g, gt, conv_w, a_row, a_col, dt_row, dt_col):
    bn, s, _ = p.shape
    L = G_CHUNK
    n = s // L
    w3 = 2 * D_QK + D_V
    rh = L // P_HALO
    tok = lambda w: pl.BlockSpec((None, L, w), lambda b, i: (b, i, 0))
    const = lambda shape: pl.BlockSpec(shape, lambda b, i: (0,) * len(shape))
    big = jax.ShapeDtypeStruct((bn, s, D_QK), BF16)
    egs = jax.ShapeDtypeStruct((bn, n * D_HEADS, LANE), F32)
    per_dir_specs = [tok(D_QK)] * 5 + [pl.BlockSpec((None, D_HEADS, LANE), lambda b, i: (b, i, 0))]
    per_dir_shapes = [big] * 5 + [egs]
    return pl.pallas_call(
        _gdn_prep_kernel,
        grid=(bn, n),
        in_specs=[
            pl.BlockSpec((None, L, w3), lambda b, i: (b, i, 1)),
            pl.BlockSpec((None, P_HALO, w3), lambda b, i: (b, jnp.maximum(i * rh - 1, 0), 1)),
            pl.BlockSpec((None, P_HALO, w3), lambda b, i: (b, jnp.minimum((i + 1) * rh, s // P_HALO - 1), 1)),
            pl.BlockSpec((None, L, LANE), lambda b, i: (b, i, 0)),
            pl.BlockSpec((None, LANE, L), lambda b, i: (b, 0, i)),
            const((3, w3)), const((1, LANE)), const((LANE, 1)), const((1, LANE)), const((LANE, 1)),
        ],
        out_specs=[tok(D_V)] + per_dir_specs * 2,
        out_shape=[big] + per_dir_shapes * 2,
        compiler_params=_cparams(("arbitrary", "arbitrary")),
        name="gdn_prep",
    )(p, p, p, g, gt, conv_w, a_row, a_col, dt_row, dt_col)


def _gdn_scan_kernel(vf_ref, kgf_ref, qgf_ref, kdf_ref, tf_ref, qkf_ref, egf_ref,
                     vb_ref, kgb_ref, qgb_ref, kdb_ref, tb_ref, qkb_ref, egb_ref,
                     of_ref, ob_ref, s_ref):
    @pl.when(pl.program_id(1) == 0)
    def _():
        s_ref[...] = jnp.zeros_like(s_ref)

    L = G_CHUNK
    units = []
    for d, refs in enumerate((
        (vf_ref, kgf_ref, qgf_ref, kdf_ref, tf_ref, qkf_ref, egf_ref, of_ref),
        (vb_ref, kgb_ref, qgb_ref, kdb_ref, tb_ref, qkb_ref, egb_ref, ob_ref),
    )):
        for hp in range(D_HEADS // 2):
            units.append((d * D_HEADS + 2 * hp, 2 * hp, slice(hp * 2 * L, (hp + 1) * 2 * L)) + refs)

    xs = []
    for idx, h0, ps, v_ref, kg_ref, qg_ref, kd_ref, t_ref, qkd_ref, eg_ref, o_ref in units:
        s_bd = _blockdiag(_lane_pair(s_ref[idx].astype(BF16), s_ref[idx + 1].astype(BF16)))
        xs.append(_dot(jnp.concatenate([kg_ref[:, ps], qg_ref[:, ps]], axis=0), s_bd))
    vns = []
    for x, (idx, h0, ps, v_ref, kg_ref, qg_ref, kd_ref, t_ref, qkd_ref, eg_ref, o_ref) in zip(xs, units):
        rhs = (v_ref[:, ps].astype(F32) - x[:L]).astype(BF16)
        vns.append(_pair_dot(t_ref[:, ps], rhs).astype(BF16))
    for x, vn, (idx, h0, ps, v_ref, kg_ref, qg_ref, kd_ref, t_ref, qkd_ref, eg_ref, o_ref) in zip(xs, vns, units):
        o_ref[:, ps] = x[L:] + _pair_dot(qkd_ref[:, ps], vn)
        for half in range(2):
            hs = slice((h0 + half) * D_DK, (h0 + half + 1) * D_DK)
            s_ref[idx + half] = (eg_ref[h0 + half:h0 + half + 1, :] * s_ref[idx + half]
                                 + _dot_tn(kd_ref[:, hs], vn[:, half * L:(half + 1) * L]))


def _gdn_scan(prep):
    v = prep[0]
    bn, s, _ = v.shape
    L = G_CHUNK
    n = s // L

    def specs(rev):
        im = (lambda b, i: (b, n - 1 - i, 0)) if rev else (lambda b, i: (b, i, 0))
        return [pl.BlockSpec((None, L, D_QK), im)] * 6 + [pl.BlockSpec((None, D_HEADS, LANE), im)]

    fwd_ops = [v] + list(prep[1:7])
    bwd_ops = [v] + list(prep[7:13])
    return pl.pallas_call(
        _gdn_scan_kernel,
        grid=(bn, n),
        in_specs=specs(False) + specs(True),
        out_specs=[pl.BlockSpec((None, L, D_V), lambda b, i: (b, i, 0)),
                   pl.BlockSpec((None, L, D_V), lambda b, i: (b, n - 1 - i, 0))],
        out_shape=[jax.ShapeDtypeStruct((bn, s, D_V), F32)] * 2,
        scratch_shapes=[pltpu.VMEM((2 * D_HEADS, D_DK, D_DV), F32)],
        compiler_params=_cparams(("arbitrary", "arbitrary")),
        name="gdn_scan",
    )(*fwd_ops, *bwd_ops)


def _head_rmsnorm(x, g, width):
    outs = []
    for h in range(x.shape[1] // width):
        xs = x[:, h * width:(h + 1) * width]
        outs.append(xs * lax.rsqrt(jnp.mean(xs * xs, axis=-1, keepdims=True) + EPS))
    return jnp.concatenate(outs, axis=1) * g


def _merge_kernel(x_ref, hf_ref, hb_ref, of_ref, ob_ref, mo_ref, dz_ref, cb_ref, cc_ref, cx_ref,
                  ga_ref, gb_ref, gc_ref, ccp_ref, cxp_ref, ccn_ref, cxn_ref, mod_ref,
                  mg_ref, dg_ref, cw_ref, wm_ref, wd_ref, wc_ref, wo_ref, o_ref):
    i = pl.program_id(1)
    nblk = pl.num_programs(1)
    f = lambda ref: ref[...].astype(F32)
    hm = _head_rmsnorm(hf_ref[...] + hb_ref[...], mg_ref[...], M_DV)
    out_a = _dot((_sigmoid(f(mo_ref)) * hm).astype(BF16), wm_ref[...])
    od = _head_rmsnorm(of_ref[...] + ob_ref[...], dg_ref[...], D_DV)
    out_b = _dot((od * _silu(f(dz_ref))).astype(BF16), wd_ref[...])

    u = f(cc_ref) * f(cx_ref)
    tm = u.shape[0]
    row = lax.broadcasted_iota(jnp.int32, u.shape, 0)
    prev_row = jnp.where(i > 0, f(ccp_ref)[P_HALO - 1:P_HALO, :] * f(cxp_ref)[P_HALO - 1:P_HALO, :], 0.0)
    next_row = jnp.where(i < nblk - 1, f(ccn_ref)[0:1, :] * f(cxn_ref)[0:1, :], 0.0)
    um = jnp.where(row == 0, prev_row, pltpu.roll(u, 1, 0))
    up = jnp.where(row == tm - 1, next_row, pltpu.roll(u, tm - 1, 0))
    conv = um * cw_ref[0:1, :] + u * cw_ref[1:2, :] + up * cw_ref[2:3, :]
    out_c = _dot((f(cb_ref) * conv).astype(BF16), wc_ref[...])

    merged = _sigmoid(f(ga_ref)) * out_a + _sigmoid(f(gb_ref)) * out_b + _sigmoid(f(gc_ref)) * out_c
    o_ref[...] = x_ref[...] + mod_ref[2:3, :] * _dot(merged.astype(BF16), wo_ref[...])


def _merge(x, hf, hb, of, ob, p, mod, m_norm_g, dn_norm_g, cm_conv_w, w_br_m, w_br_d, w_br_c, w_out):
    bn, s, _ = x.shape
    tm = min(s, 256)
    rh = tm // P_HALO
    tok = pl.BlockSpec((None, tm, D_MODEL), lambda b, i: (b, i, 0))
    pblk = lambda j: pl.BlockSpec((None, tm, D_MODEL), lambda b, i: (b, i, j))
    prev = lambda j: pl.BlockSpec((None, P_HALO, D_MODEL), lambda b, i: (b, jnp.maximum(i * rh - 1, 0), j))
    nxt = lambda j: pl.BlockSpec((None, P_HALO, D_MODEL),
                                 lambda b, i: (b, jnp.minimum((i + 1) * rh, s // P_HALO - 1), j))
    const = lambda shape: pl.BlockSpec(shape, lambda b, i: (0,) * len(shape))
    wspec = const((D_MODEL, D_MODEL))
    return pl.pallas_call(
        _merge_kernel,
        grid=(bn, s // tm),
        in_specs=[tok] * 5 + [pblk(j) for j in (BLK_MO, BLK_DZ, BLK_CB, BLK_CC, BLK_CX, BLK_GA, BLK_GB, BLK_GC)]
        + [prev(BLK_CC), prev(BLK_CX), nxt(BLK_CC), nxt(BLK_CX)]
        + [pl.BlockSpec((None, 6, D_MODEL), lambda b, i: (b, 0, 0))]
        + [const((1, D_MODEL)), const((1, D_MODEL)), const((3, D_MODEL))] + [wspec] * 4,
        out_specs=tok,
        out_shape=jax.ShapeDtypeStruct((bn, s, D_MODEL), F32),
        compiler_params=_cparams(("arbitrary", "arbitrary")),
        name="merge",
    )(x, hf, hb, of, ob, p, p, p, p, p, p, p, p, p, p, p, p, mod, m_norm_g, dn_norm_g, cm_conv_w,
      w_br_m, w_br_d, w_br_c, w_out)


FF_TILE = 256


def _ffn_kernel(x_ref, xp_ref, xn_ref, mod_ref, g_ref, wu_ref, cw_ref, wd_ref, fg_ref, o_ref, acc_ref, *, final):
    i = pl.program_id(1)
    nblk = pl.num_programs(1)
    x = x_ref[...]
    tm = x.shape[0]
    xa = jnp.concatenate([xp_ref[...], x, xn_ref[...]], axis=0)
    hf = _norm_mod(xa, g_ref[...], mod_ref[4:5, :], mod_ref[3:4, :])
    h = hf.astype(BF16)
    hmid = hf[8:8 + tm].astype(BF16)
    rows = tm + 16
    row = lax.broadcasted_iota(jnp.int32, (rows, FF_TILE), 0)
    kill_prev = jnp.logical_and(row == 7, i == 0)
    kill_next = jnp.logical_and(row == tm + 8, i == nblk - 1)
    kill = jnp.logical_or(kill_prev, kill_next)
    for c in range(D_FF // FF_TILE):
        cs = slice(c * FF_TILE, (c + 1) * FF_TILE)
        a = _dot(h, wu_ref[:, cs])
        a = jnp.where(kill, 0.0, a)
        b = _dot(hmid, wu_ref[:, D_FF + c * FF_TILE:D_FF + (c + 1) * FF_TILE])
        conv = (pltpu.roll(a, 1, 0) * cw_ref[0:1, cs] + a * cw_ref[1:2, cs]
                + pltpu.roll(a, rows - 1, 0) * cw_ref[2:3, cs])[8:8 + tm]
        y = (_silu(conv) * b).astype(BF16)
        contrib = _dot(y, wd_ref[cs, :])
        if c == 0:
            acc_ref[...] = contrib
        else:
            acc_ref[...] += contrib
    out = x + mod_ref[5:6, :] * acc_ref[...]
    if final:
        out = (out * lax.rsqrt(jnp.mean(out * out, axis=-1, keepdims=True) + EPS)) * fg_ref[...]
    o_ref[...] = out


def _ffn(x, mod, norm_g, w_up, conv_w, w_down, final_g, final):
    bn, s, _ = x.shape
    tm = min(s, 512)
    r8 = tm // 8
    const = lambda shape: pl.BlockSpec(shape, lambda b, i: (0,) * len(shape))
    return pl.pallas_call(
        functools.partial(_ffn_kernel, final=final),
        grid=(bn, s // tm),
        in_specs=[
            pl.BlockSpec((None, tm, D_MODEL), lambda b, i: (b, i, 0)),
            pl.BlockSpec((None, 8, D_MODEL), lambda b, i: (b, jnp.maximum(i * r8 - 1, 0), 0)),
            pl.BlockSpec((None, 8, D_MODEL), lambda b, i: (b, jnp.minimum((i + 1) * r8, s // 8 - 1), 0)),
            pl.BlockSpec((None, 6, D_MODEL), lambda b, i: (b, 0, 0)),
            const((1, D_MODEL)), const((D_MODEL, 2 * D_FF)), const((3, D_FF)), const((D_FF, D_MODEL)),
            const((1, D_MODEL)),
        ],
        out_specs=pl.BlockSpec((None, tm, D_MODEL), lambda b, i: (b, i, 0)),
        out_shape=jax.ShapeDtypeStruct((bn, s, D_MODEL), F32),
        scratch_shapes=[pltpu.VMEM((tm, D_MODEL), F32)],
        compiler_params=_cparams(("arbitrary", "arbitrary")),
        name="ffn_final" if final else "ffn",
    )(x, x, x, mod, norm_g, w_up, conv_w, w_down, final_g)


def _prepare(w_in, m_gate_b, m_norm_g, dn_a_log, dn_dt_bias, dn_norm_g):
    bounds = [0]
    for cw in COLS:
        bounds.append(bounds[-1] + cw)
    seg = [w_in[:, :, bounds[j]:bounds[j + 1]] for j in range(len(COLS))]
    m_q, m_k, m_v, m_o, m_if, d_qkv, d_z, d_ba, c_bcx, br_g = seg
    w_main = jnp.concatenate([m_q * (M_DQK ** -0.5), m_k, m_v, m_o, d_qkv, d_z, c_bcx, br_g], axis=-1).astype(BF16)
    pad = jnp.zeros(w_in.shape[:2] + (LANE - 4 * M_HEADS - 4 * D_HEADS,), w_in.dtype)
    w_gate = jnp.concatenate([m_if, d_ba, pad], axis=-1).astype(BF16)
    w_gate_t = jnp.swapaxes(w_gate, 1, 2)

    lanes = lambda t, off: jnp.pad(t, ((0, 0), (off, LANE - off - t.shape[1])))
    bias = lanes(m_gate_b, GATE_M)
    neg_a = -jnp.exp(dn_a_log.astype(F32))
    zeros_h = jnp.zeros((DEPTH, D_HEADS), F32)
    a_l = lanes(jnp.concatenate([zeros_h, neg_a[:, 0], zeros_h, neg_a[:, 1]], axis=-1), GATE_D)
    dtb = dn_dt_bias.astype(F32)
    dt_l = lanes(jnp.concatenate([zeros_h, dtb[:, 0], zeros_h, dtb[:, 1]], axis=-1), GATE_D)
    as_row = lambda t: t[:, None, :]
    as_col = lambda t: t[:, :, None]
    return dict(
        w_main=w_main, w_gate=w_gate, w_gate_t=w_gate_t,
        bias_r=as_row(bias), bias_c=as_col(bias),
        a_r=as_row(a_l), a_c=as_col(a_l), dt_r=as_row(dt_l), dt_c=as_col(dt_l),
        m_norm=jnp.tile(m_norm_g, (1, M_HEADS))[:, None, :], dn_norm=jnp.tile(dn_norm_g, (1, D_HEADS))[:, None, :],
    )


def _trunk(x, c, prm, norm1_g, w_ada, b_ada, dn_conv_w, cm_conv_w, w_br_m, w_br_d, w_br_c, w_out, norm2_g,
           w_up, ffn_conv_w, w_down, final_g):
    mod = _modulation(c, w_ada, b_ada)
    fg = final_g[None, :]
    for l in range(DEPTH):
        p, g, gt = _proj_in(x, mod[l], norm1_g[l][None, :], prm["w_main"][l], prm["w_gate"][l], prm["w_gate_t"][l])
        hf, hb = _mlstm(p, g, gt, prm["bias_r"][l], prm["bias_c"][l])
        prep = _gdn_prep(p, g, gt, dn_conv_w[l], prm["a_r"][l], prm["a_c"][l], prm["dt_r"][l], prm["dt_c"][l])
        of, ob = _gdn_scan(prep)
        x = _merge(x, hf, hb, of, ob, p, mod[l], prm["m_norm"][l], prm["dn_norm"][l], cm_conv_w[l],
                   w_br_m[l], w_br_d[l], w_br_c[l], w_out[l])
        x = _ffn(x, mod[l], norm2_g[l][None, :], w_up[l], ffn_conv_w[l], w_down[l], fg, l == DEPTH - 1)
    return x


def kernel(x_prompt, x_sample, c_prompt, c_sample, norm1_g, w_ada, b_ada, w_in, m_gate_b, m_norm_g, dn_conv_w, dn_a_log, dn_dt_bias, dn_norm_g, cm_conv_w, w_br_m, w_br_d, w_br_c, w_out, norm2_g, w_up, ffn_conv_w, w_down, final_g):
    prm = _prepare(w_in, m_gate_b, m_norm_g, dn_a_log, dn_dt_bias, dn_norm_g)
    bf = lambda t: t.astype(BF16)
    args = (prm, norm1_g, w_ada, b_ada, dn_conv_w, cm_conv_w, bf(w_br_m), bf(w_br_d), bf(w_br_c), bf(w_out),
            norm2_g, bf(w_up), ffn_conv_w, bf(w_down), final_g)
    return (_trunk(x_prompt, c_prompt, *args), _trunk(x_sample, c_sample, *args))
```

```python
import functools

import jax
import jax.numpy as jnp
from jax import lax
from jax.experimental import pallas as pl
from jax.experimental.pallas import tpu as pltpu

F32 = jnp.float32
BF16 = jnp.bfloat16

D_MODEL = 1024
DEPTH = 4
M_HEADS = 4
M_DQK = 128
M_DV = 256
M_CHUNK = 128
D_HEADS = 8
D_DK = 128
D_DV = 128
G_CHUNK = 128
D_FF = 2816
EPS = 1e-6

M_QK = M_HEADS * M_DQK
M_V = M_HEADS * M_DV
D_QK = D_HEADS * D_DK
D_V = D_HEADS * D_DV
COLS = (M_QK, M_QK, M_V, M_V, 4 * M_HEADS, 2 * D_QK + D_V, D_V, 4 * D_HEADS, 3 * D_MODEL, 3 * D_MODEL)

LANE = 128
P_MAIN = 13 * 1024
BLK_MO, BLK_DZ, BLK_CB, BLK_CC, BLK_CX, BLK_GA, BLK_GB, BLK_GC = 2, 6, 7, 8, 9, 10, 11, 12
GATE_M = 0
GATE_D = 4 * M_HEADS

M_BATCH = 1
P_DTYPE = BF16
P_HALO = 16
VMEM_LIMIT = 56 * 1024 * 1024


def _cparams(sem):
    return pltpu.CompilerParams(dimension_semantics=sem, vmem_limit_bytes=VMEM_LIMIT)


def _sigmoid(x):
    return 0.5 * jnp.tanh(0.5 * x) + 0.5


def _silu(x):
    return x * _sigmoid(x)


def _softplus(x):
    return jnp.maximum(x, 0.0) + jnp.log(1.0 + jnp.exp(-jnp.abs(x)))


def _dot(a, b):
    return jnp.dot(a, b, preferred_element_type=F32)


def _dot_nt(a, b):
    return lax.dot_general(a, b, (((1,), (1,)), ((), ())), preferred_element_type=F32)


def _dot_tn(a, b):
    return lax.dot_general(a, b, (((0,), (0,)), ((), ())), preferred_element_type=F32)


def _dot_hi(a, b):
    return jnp.dot(a, b, preferred_element_type=F32, precision=lax.Precision.HIGHEST)


def _split3(x):
    h1 = x.astype(BF16)
    r1 = x - h1.astype(F32)
    h2 = r1.astype(BF16)
    h3 = (r1 - h2.astype(F32)).astype(BF16)
    return h1, h2, h3


def _cumsum_cols(mask01, x):
    h1, h2, h3 = _split3(x)
    return _dot(mask01, h1) + (_dot(mask01, h2) + _dot(mask01, h3))


def _cumsum_rows(x, mask01):
    h1, h2, h3 = _split3(x)
    return _dot(h1, mask01) + (_dot(h2, mask01) + _dot(h3, mask01))


def _scan_masks(n, reverse):
    r = lax.broadcasted_iota(jnp.int32, (n, n), 0)
    c = lax.broadcasted_iota(jnp.int32, (n, n), 1)
    if reverse:
        return c >= r, c > r, r >= c
    return c <= r, c < r, r <= c


def _mod_kernel(c_ref, w_ref, b_ref, o_ref):
    o_ref[...] = _dot_hi(_silu(c_ref[...]), w_ref[...]) + b_ref[...]


def _modulation(c, w_ada, b_ada):
    bn = c.shape[0]
    out = pl.pallas_call(
        _mod_kernel,
        grid=(DEPTH, 6),
        in_specs=[
            pl.BlockSpec((bn, D_MODEL), lambda l, j: (0, 0)),
            pl.BlockSpec((None, D_MODEL, D_MODEL), lambda l, j: (l, 0, j)),
            pl.BlockSpec((None, 1, D_MODEL), lambda l, j: (l, 0, j)),
        ],
        out_specs=pl.BlockSpec((None, bn, D_MODEL), lambda l, j: (l, 0, j)),
        out_shape=jax.ShapeDtypeStruct((DEPTH, bn, 6 * D_MODEL), F32),
        compiler_params=_cparams(("arbitrary", "arbitrary")),
        name="adaln_mod",
    )(c, w_ada, b_ada.reshape(DEPTH, 1, 6 * D_MODEL))
    return out.reshape(DEPTH, bn, 6, D_MODEL)


def _norm_mod(x, g, scale, shift):
    y = x * lax.rsqrt(jnp.mean(x * x, axis=-1, keepdims=True) + EPS)
    return (y * g) * (1.0 + scale) + shift


PROJ_SUB = 256


def _proj_kernel(x_ref, mod_ref, g_ref, w_ref, wg_ref, wgt_ref, o_ref, og_ref, ogt_ref, h_ref):
    @pl.when(pl.program_id(2) == 0)
    def _():
        h = _norm_mod(x_ref[...], g_ref[...], mod_ref[1:2, :], mod_ref[0:1, :]).astype(BF16)
        h_ref[...] = h
        og_ref[...] = _dot(h, wg_ref[...])
        ogt_ref[...] = _dot_nt(wgt_ref[...], h)

    h = h_ref[...]
    for c in range(o_ref.shape[1] // PROJ_SUB):
        cs = slice(c * PROJ_SUB, (c + 1) * PROJ_SUB)
        o_ref[:, cs] = _dot(h, w_ref[:, cs]).astype(o_ref.dtype)


def _proj_in(x, mod, norm_g, w_main, w_gate, w_gate_t):
    bn, s, _ = x.shape
    tm = min(s, 1024)
    tn = P_MAIN // 4
    return pl.pallas_call(
        _proj_kernel,
        grid=(bn, s // tm, P_MAIN // tn),
        in_specs=[
            pl.BlockSpec((None, tm, D_MODEL), lambda b, i, j: (b, i, 0)),
            pl.BlockSpec((None, 6, D_MODEL), lambda b, i, j: (b, 0, 0)),
            pl.BlockSpec((1, D_MODEL), lambda b, i, j: (0, 0)),
            pl.BlockSpec((D_MODEL, tn), lambda b, i, j: (0, j)),
            pl.BlockSpec((D_MODEL, LANE), lambda b, i, j: (0, 0)),
            pl.BlockSpec((LANE, D_MODEL), lambda b, i, j: (0, 0)),
        ],
        out_specs=[
            pl.BlockSpec((None, tm, tn), lambda b, i, j: (b, i, j)),
            pl.BlockSpec((None, tm, LANE), lambda b, i, j: (b, i, 0)),
            pl.BlockSpec((None, LANE, tm), lambda b, i, j: (b, 0, i)),
        ],
        out_shape=[
            jax.ShapeDtypeStruct((bn, s, P_MAIN), P_DTYPE),
            jax.ShapeDtypeStruct((bn, s, LANE), F32),
            jax.ShapeDtypeStruct((bn, LANE, s), F32),
        ],
        scratch_shapes=[pltpu.VMEM((tm, D_MODEL), BF16)],
        compiler_params=_cparams(("arbitrary", "arbitrary", "arbitrary")),
        name="proj_in",
    )(x, mod, norm_g, w_main, w_gate, w_gate_t)


def _log_sigmoid(x):
    return jnp.minimum(x, 0.0) - jnp.log(1.0 + jnp.exp(-jnp.abs(x)))


def _mlstm_kernel(qf_ref, kf_ref, vf_ref, gf_ref, gtf_ref, qb_ref, kb_ref, vb_ref, gb_ref, gtb_ref,
                  bias_r_ref, bias_c_ref, hf_ref, hb_ref, c_ref, m_ref):
    @pl.when(pl.program_id(1) == 0)
    def _():
        c_ref[...] = jnp.zeros_like(c_ref)
        m_ref[...] = jnp.zeros_like(m_ref)


    L = M_CHUNK
    dirs = (
        (qf_ref, kf_ref, vf_ref, gf_ref, gtf_ref, hf_ref, False, L - 1, 0),
        (qb_ref, kb_ref, vb_ref, gb_ref, gtb_ref, hb_ref, True, 0, 2),
    )
    units = []
    for bi, d in [(bi, d) for bi in range(M_BATCH) for d in range(2)]:
        q_ref, k_ref, v_ref, g_ref, gt_ref, h_ref, reverse, last, gbase = dirs[d]
        mask, _, mask_t = _scan_masks(L, reverse)
        gc = g_ref[bi] + bias_r_ref[...]
        gr = gt_ref[bi] + bias_c_ref[...]
        bc_all = _cumsum_cols(mask.astype(BF16), _log_sigmoid(gc))
        br_all = _cumsum_rows(_log_sigmoid(gr), mask_t.astype(BF16))
        for h in range(M_HEADS):
            ci = GATE_M + gbase * M_HEADS + h
            cf = ci + M_HEADS
            idx = (bi * 2 + d) * M_HEADS + h
            k = k_ref[bi, :, h * M_DQK:(h + 1) * M_DQK].astype(F32)
            vb = jnp.concatenate([v_ref[bi, :, h * M_DV:(h + 1) * M_DV].astype(BF16), jnp.ones((L, LANE), BF16)],
                                 axis=1)
            units.append(dict(
                idx=idx, mask=mask, last=last, h_ref=h_ref, bi=bi, h=h,
                b_col=bc_all[:, cf:cf + 1], b_row=br_all[cf:cf + 1, :], i_col=gc[:, ci:ci + 1], i_row=gr[ci:ci + 1, :],
                m_old=m_ref[idx:idx + 1, 0:1], c_old=c_ref[idx],
                k=k, qb=q_ref[bi, :, h * M_DQK:(h + 1) * M_DQK].astype(BF16), vb=vb))

    for u in units:
        u["qk"] = _dot_nt(u["qb"], u["k"].astype(BF16))
        u["qc"] = _dot(u["qb"], u["c_old"].astype(BF16))
    for u in units:
        a_col = u["b_col"] + u["m_old"]
        dm = jnp.where(u["mask"], u["b_col"] - u["b_row"] + u["i_row"], -jnp.inf)
        mt = jnp.maximum(a_col, jnp.max(dm, axis=-1, keepdims=True))
        u["floor"] = jnp.exp(-jnp.broadcast_to(mt, (L, LANE)))
        u["s"] = u["qk"] * jnp.exp(dm - mt)
        u["sa"] = jnp.exp(a_col - mt)
    for u in units:
        u["sv"] = _dot(u["s"].astype(BF16), u["vb"])
    for u in units:
        num = u["sa"] * u["qc"] + u["sv"]
        inv = 1.0 / jnp.maximum(jnp.abs(num[:, M_DV:]), u["floor"])
        for t in range(M_DV // LANE):
            lo = u["h"] * M_DV + t * LANE
            u["h_ref"][u["bi"], :, lo:lo + LANE] = num[:, t * LANE:(t + 1) * LANE] * inv
    for u in units:
        idx, last = u["idx"], u["last"]
        b_tot = u["b_col"][last:last + 1, :]
        lw_col = b_tot - u["b_col"] + u["i_col"]
        lw_row = b_tot - u["b_row"] + u["i_row"]
        m_new = jnp.maximum(b_tot + u["m_old"], jnp.max(lw_row, axis=-1, keepdims=True))
        dec = jnp.exp(b_tot + u["m_old"] - m_new)
        wkk = jnp.exp(lw_col - m_new) * u["k"]
        c_ref[idx] = dec * u["c_old"] + _dot_tn(wkk.astype(BF16), u["vb"])
        m_ref[idx:idx + 1, :] = jnp.broadcast_to(m_new, (1, LANE))


def _mlstm(p, g, gt, bias_r, bias_c):
    bn, s, _ = p.shape
    L = M_CHUNK
    n = s // L
    mb = M_BATCH
    blk = lambda w, j, rev: pl.BlockSpec((mb, L, w), (lambda b, i: (b, n - 1 - i, j)) if rev else (lambda b, i: (b, i, j)))
    gtblk = lambda rev: pl.BlockSpec((mb, LANE, L), (lambda b, i: (b, 0, n - 1 - i)) if rev else (lambda b, i: (b, 0, i)))
    in_specs = []
    for rev in (False, True):
        in_specs += [blk(M_QK, 0, rev), blk(M_QK, 1, rev), blk(M_V, 1, rev), blk(LANE, 0, rev), gtblk(rev)]
    in_specs += [pl.BlockSpec((1, LANE), lambda b, i: (0, 0)), pl.BlockSpec((LANE, 1), lambda b, i: (0, 0))]
    n_units = mb * 2 * M_HEADS
    return pl.pallas_call(
        _mlstm_kernel,
        grid=(bn // mb, n),
        in_specs=in_specs,
        out_specs=[blk(M_V, 0, False), blk(M_V, 0, True)],
        out_shape=[jax.ShapeDtypeStruct((bn, s, M_V), F32)] * 2,
        scratch_shapes=[
            pltpu.VMEM((n_units, M_DQK, M_DV + LANE), F32),
            pltpu.VMEM((n_units, LANE), F32),
        ],
        compiler_params=_cparams(("arbitrary", "arbitrary")),
        name="mlstm",
    )(p, p, p, g, gt, p, p, p, g, gt, bias_r, bias_c)


G_SCAN_BATCH = 2
G_PREP_CHUNKS = 2
INV_BASE = 8


def _lane_pair(a, b):
    return jnp.concatenate([a, b], axis=1)


def _blockdiag(pair):
    n = pair.shape[0]
    z = jnp.zeros((n, n), pair.dtype)
    return jnp.concatenate([_lane_pair(pair[:, :n], z), _lane_pair(z, pair[:, n:])], axis=0)


def _pair_dot(lhs, rhs):
    return _dot(lhs, _blockdiag(rhs))


def _unit_tri_inverse_pairs(a_list):
    n = a_list[0].shape[0]
    r = lax.broadcasted_iota(jnp.int32, (n, 2 * n), 0)
    c = lax.broadcasted_iota(jnp.int32, (n, 2 * n), 1) % n
    same_block = lambda size: (r // size) == (c // size)
    eye = (r == c).astype(F32)
    bf = lambda xs: [x.astype(BF16) for x in xs]
    a0 = [jnp.where(same_block(INV_BASE), a, 0.0) for a in a_list]
    a0b = bf(a0)
    a2 = [_pair_dot(x, x) for x in a0b]
    a2b = bf(a2)
    a3 = [_pair_dot(x2, x) for x2, x in zip(a2b, a0b)]
    a4 = [_pair_dot(x2, x2) for x2 in a2b]
    p = [eye - x + x2 - x3 for x, x2, x3 in zip(a0, a2, a3)]
    t = [pp + _pair_dot(pb, x4) for pp, pb, x4 in zip(p, bf(p), bf(a4))]
    size = INV_BASE
    while size < n:
        level = jnp.logical_and(same_block(2 * size), jnp.logical_not(same_block(size)))
        tb = bf(t)
        x = [_pair_dot(jnp.where(level, a, 0.0).astype(BF16), tt) for a, tt in zip(a_list, tb)]
        t = [tt - _pair_dot(tq, xx) for tt, tq, xx in zip(t, tb, bf(x))]
        size *= 2
    return t


def _gdn_prep_kernel(x_ref, xp_ref, xn_ref, g_ref, gt_ref, cw_ref, ar_ref, ac_ref, dr_ref, dc_ref,
                     v_ref, kgf_ref, qgf_ref, kdf_ref, tf_ref, qkf_ref, egf_ref,
                     kgb_ref, qgb_ref, kdb_ref, tb_ref, qkb_ref, egb_ref):
    i = pl.program_id(1)
    nblk = pl.num_programs(1)
    L = G_CHUNK
    rows = x_ref.shape[0]
    x = x_ref[...].astype(F32)
    row = lax.broadcasted_iota(jnp.int32, x.shape, 0)
    prev_row = jnp.where(i > 0, xp_ref[P_HALO - 1:P_HALO, :].astype(F32), 0.0)
    next_row = jnp.where(i < nblk - 1, xn_ref[0:1, :].astype(F32), 0.0)
    xm = jnp.where(row == 0, prev_row, pltpu.roll(x, 1, 0))
    xq = jnp.where(row == rows - 1, next_row, pltpu.roll(x, rows - 1, 0))
    y = _silu(xm * cw_ref[0:1, :] + x * cw_ref[1:2, :] + xq * cw_ref[2:3, :])
    v_ref[...] = y[:, 2 * D_QK:].astype(v_ref.dtype)

    gc_all = g_ref[...]
    gr_all = gt_ref[...]
    beta_c_all = _sigmoid(gc_all)
    beta_r_all = _sigmoid(gr_all)
    lg_c_all = ar_ref[...] * _softplus(gc_all + dr_ref[...])
    lg_r_all = ac_ref[...] * _softplus(gr_all + dc_ref[...])

    units = []
    amats = []
    for ck in range(rows // L):
        rs = slice(ck * L, (ck + 1) * L)
        beta_c, beta_r, lg_c, lg_r = beta_c_all[rs], beta_r_all[:, rs], lg_c_all[rs], lg_r_all[:, rs]
        qn, kn = [], []
        for h in range(D_HEADS):
            qh = y[rs, h * D_DK:(h + 1) * D_DK]
            kh = y[rs, D_QK + h * D_DK:D_QK + (h + 1) * D_DK]
            qn.append(qh * lax.rsqrt(jnp.sum(qh * qh, axis=-1, keepdims=True) + EPS) * (D_DK ** -0.5))
            kn.append(kh * lax.rsqrt(jnp.sum(kh * kh, axis=-1, keepdims=True) + EPS))
        kk_p, qk_p = [], []
        for hp in range(D_HEADS // 2):
            kp = _lane_pair(kn[2 * hp], kn[2 * hp + 1]).astype(BF16)
            qp = _lane_pair(qn[2 * hp], qn[2 * hp + 1]).astype(BF16)
            kbd = _blockdiag(kp)
            kk_p.append(_dot_nt(kp, kbd))
            qk_p.append(_dot_nt(qp, kbd))

        for reverse, last, gbase, refs in (
            (False, L - 1, 0, (kgf_ref, qgf_ref, kdf_ref, tf_ref, qkf_ref, egf_ref)),
            (True, 0, 2, (kgb_ref, qgb_ref, kdb_ref, tb_ref, qkb_ref, egb_ref)),
        ):
            mask, strict, mask_t = _scan_masks(L, reverse)
            cum_c = _cumsum_cols(mask.astype(BF16), lg_c)
            cum_r = _cumsum_rows(lg_r, mask_t.astype(BF16))
            kg_ref, qg_ref, kd_ref, t_ref, qkd_ref, eg_ref = refs
            for hp in range(D_HEADS // 2):
                decays, a_halves, b_rows = [], [], []
                for half, h in enumerate((2 * hp, 2 * hp + 1)):
                    cb = GATE_D + gbase * D_HEADS + h
                    ca = cb + D_HEADS
                    gcum_c = cum_c[:, ca:ca + 1]
                    decay = jnp.exp(jnp.where(mask, gcum_c - cum_r[ca:ca + 1, :], -jnp.inf))
                    kk = kk_p[hp][:, half * L:(half + 1) * L]
                    a_halves.append(jnp.where(strict, (beta_c[:, cb:cb + 1] * kk) * decay, 0.0))
                    decays.append(decay)
                    b_rows.append(beta_r[cb:cb + 1, :])
                    g_tot = gcum_c[last:last + 1, :]
                    eg = jnp.exp(gcum_c)
                    hs = slice(h * D_DK, (h + 1) * D_DK)
                    kg_ref[rs, hs] = (kn[h] * eg).astype(kg_ref.dtype)
                    qg_ref[rs, hs] = (qn[h] * eg).astype(qg_ref.dtype)
                    kd_ref[rs, hs] = (kn[h] * jnp.exp(g_tot - gcum_c)).astype(kd_ref.dtype)
                    er = ck * D_HEADS + h
                    eg_ref[er:er + 1, :] = jnp.broadcast_to(jnp.exp(g_tot), (1, LANE))
                ps = slice(hp * 2 * L, (hp + 1) * 2 * L)
                qkd_ref[rs, ps] = (qk_p[hp] * _lane_pair(*decays)).astype(qkd_ref.dtype)
                amats.append(_lane_pair(*a_halves))
                units.append((t_ref, rs, ps, _lane_pair(*b_rows)))

    for (t_ref, rs, ps, b_row), tmat in zip(units, _unit_tri_inverse_pairs(amats)):
        t_ref[rs, ps] = (tmat * b_row).astype(t_ref.dtype)


def _gdn_prep(p, g, gt, conv_w, a_row, a_col, dt_row, dt_col):
    bn, s, _ = p.shape
    L = G_CHUNK
    n = s // L
    w3 = 2 * D_QK + D_V
    nck = min(G_PREP_CHUNKS, n)
    rows = nck * L
    rh = rows // P_HALO
    tok = lambda w: pl.BlockSpec((None, rows, w), lambda b, i: (b, i, 0))
    const = lambda shape: pl.BlockSpec(shape, lambda b, i: (0,) * len(shape))
    big = jax.ShapeDtypeStruct((bn, s, D_QK), BF16)
    egs = jax.ShapeDtypeStruct((bn, n * D_HEADS, LANE), F32)
    per_dir_specs = [tok(D_QK)] * 5 + [pl.BlockSpec((None, nck * D_HEADS, LANE), lambda b, i: (b, i, 0))]
    per_dir_shapes = [big] * 5 + [egs]
    return pl.pallas_call(
        _gdn_prep_kernel,
        grid=(bn, n // nck),
        in_specs=[
            pl.BlockSpec((None, rows, w3), lambda b, i: (b, i, 1)),
            pl.BlockSpec((None, P_HALO, w3), lambda b, i: (b, jnp.maximum(i * rh - 1, 0), 1)),
            pl.BlockSpec((None, P_HALO, w3), lambda b, i: (b, jnp.minimum((i + 1) * rh, s // P_HALO - 1), 1)),
            pl.BlockSpec((None, rows, LANE), lambda b, i: (b, i, 0)),
            pl.BlockSpec((None, LANE, rows), lambda b, i: (b, 0, i)),
            const((3, w3)), const((1, LANE)), const((LANE, 1)), const((1, LANE)), const((LANE, 1)),
        ],
        out_specs=[tok(D_V)] + per_dir_specs * 2,
        out_shape=[big] + per_dir_shapes * 2,
        compiler_params=_cparams(("arbitrary", "arbitrary")),
        name="gdn_prep",
    )(p, p, p, g, gt, conv_w, a_row, a_col, dt_row, dt_col)


def _gdn_scan_kernel(vf_ref, kgf_ref, qgf_ref, kdf_ref, tf_ref, qkf_ref, egf_ref,
                     vb_ref, kgb_ref, qgb_ref, kdb_ref, tb_ref, qkb_ref, egb_ref,
                     of_ref, ob_ref, s_ref):
    @pl.when(pl.program_id(1) == 0)
    def _():
        s_ref[...] = jnp.zeros_like(s_ref)

    L = G_CHUNK
    units = []
    for bi in range(vf_ref.shape[0]):
        for d, refs in enumerate((
            (vf_ref, kgf_ref, qgf_ref, kdf_ref, tf_ref, qkf_ref, egf_ref, of_ref),
            (vb_ref, kgb_ref, qgb_ref, kdb_ref, tb_ref, qkb_ref, egb_ref, ob_ref),
        )):
            for hp in range(D_HEADS // 2):
                idx = (bi * 2 + d) * D_HEADS + 2 * hp
                units.append((bi, idx, 2 * hp, slice(hp * 2 * L, (hp + 1) * 2 * L)) + refs)

    xs = []
    for bi, idx, h0, ps, v_ref, kg_ref, qg_ref, kd_ref, t_ref, qkd_ref, eg_ref, o_ref in units:
        s_bd = _blockdiag(_lane_pair(s_ref[idx].astype(BF16), s_ref[idx + 1].astype(BF16)))
        xs.append(_dot(jnp.concatenate([kg_ref[bi, :, ps], qg_ref[bi, :, ps]], axis=0), s_bd))
    vns = []
    for x, (bi, idx, h0, ps, v_ref, kg_ref, qg_ref, kd_ref, t_ref, qkd_ref, eg_ref, o_ref) in zip(xs, units):
        rhs = (v_ref[bi, :, ps].astype(F32) - x[:L]).astype(BF16)
        vns.append(_pair_dot(t_ref[bi, :, ps], rhs).astype(BF16))
    for x, vn, (bi, idx, h0, ps, v_ref, kg_ref, qg_ref, kd_ref, t_ref, qkd_ref, eg_ref, o_ref) in zip(xs, vns, units):
        o_ref[bi, :, ps] = x[L:] + _pair_dot(qkd_ref[bi, :, ps], vn)
        for half in range(2):
            hs = slice((h0 + half) * D_DK, (h0 + half + 1) * D_DK)
            s_ref[idx + half] = (eg_ref[bi, h0 + half:h0 + half + 1, :] * s_ref[idx + half]
                                 + _dot_tn(kd_ref[bi, :, hs], vn[:, half * L:(half + 1) * L]))


def _gdn_scan(prep):
    v = prep[0]
    bn, s, _ = v.shape
    L = G_CHUNK
    n = s // L

    sb = G_SCAN_BATCH

    def specs(rev):
        im = (lambda b, i: (b, n - 1 - i, 0)) if rev else (lambda b, i: (b, i, 0))
        return [pl.BlockSpec((sb, L, D_QK), im)] * 6 + [pl.BlockSpec((sb, D_HEADS, LANE), im)]

    fwd_ops = [v] + list(prep[1:7])
    bwd_ops = [v] + list(prep[7:13])
    return pl.pallas_call(
        _gdn_scan_kernel,
        grid=(bn // sb, n),
        in_specs=specs(False) + specs(True),
        out_specs=[pl.BlockSpec((sb, L, D_V), lambda b, i: (b, i, 0)),
                   pl.BlockSpec((sb, L, D_V), lambda b, i: (b, n - 1 - i, 0))],
        out_shape=[jax.ShapeDtypeStruct((bn, s, D_V), F32)] * 2,
        scratch_shapes=[pltpu.VMEM((sb * 2 * D_HEADS, D_DK, D_DV), F32)],
        compiler_params=_cparams(("arbitrary", "arbitrary")),
        name="gdn_scan",
    )(*fwd_ops, *bwd_ops)


def _head_rmsnorm(x, g, width):
    outs = []
    for h in range(x.shape[1] // width):
        xs = x[:, h * width:(h + 1) * width]
        outs.append(xs * lax.rsqrt(jnp.mean(xs * xs, axis=-1, keepdims=True) + EPS))
    return jnp.concatenate(outs, axis=1) * g


def _merge_kernel(x_ref, hf_ref, hb_ref, of_ref, ob_ref, mo_ref, dz_ref, cb_ref, cc_ref, cx_ref,
                  ga_ref, gb_ref, gc_ref, ccp_ref, cxp_ref, ccn_ref, cxn_ref, mod_ref,
                  mg_ref, dg_ref, cw_ref, wm_ref, wd_ref, wc_ref, wo_ref, o_ref):
    i = pl.program_id(1)
    nblk = pl.num_programs(1)
    f = lambda ref: ref[...].astype(F32)
    hm = _head_rmsnorm(hf_ref[...] + hb_ref[...], mg_ref[...], M_DV)
    out_a = _dot((_sigmoid(f(mo_ref)) * hm).astype(BF16), wm_ref[...])
    od = _head_rmsnorm(of_ref[...] + ob_ref[...], dg_ref[...], D_DV)
    out_b = _dot((od * _silu(f(dz_ref))).astype(BF16), wd_ref[...])

    u = f(cc_ref) * f(cx_ref)
    tm = u.shape[0]
    row = lax.broadcasted_iota(jnp.int32, u.shape, 0)
    prev_row = jnp.where(i > 0, f(ccp_ref)[P_HALO - 1:P_HALO, :] * f(cxp_ref)[P_HALO - 1:P_HALO, :], 0.0)
    next_row = jnp.where(i < nblk - 1, f(ccn_ref)[0:1, :] * f(cxn_ref)[0:1, :], 0.0)
    um = jnp.where(row == 0, prev_row, pltpu.roll(u, 1, 0))
    up = jnp.where(row == tm - 1, next_row, pltpu.roll(u, tm - 1, 0))
    conv = um * cw_ref[0:1, :] + u * cw_ref[1:2, :] + up * cw_ref[2:3, :]
    out_c = _dot((f(cb_ref) * conv).astype(BF16), wc_ref[...])

    merged = _sigmoid(f(ga_ref)) * out_a + _sigmoid(f(gb_ref)) * out_b + _sigmoid(f(gc_ref)) * out_c
    o_ref[...] = x_ref[...] + mod_ref[2:3, :] * _dot(merged.astype(BF16), wo_ref[...])


def _merge(x, hf, hb, of, ob, p, mod, m_norm_g, dn_norm_g, cm_conv_w, w_br_m, w_br_d, w_br_c, w_out):
    bn, s, _ = x.shape
    tm = min(s, 256)
    rh = tm // P_HALO
    tok = pl.BlockSpec((None, tm, D_MODEL), lambda b, i: (b, i, 0))
    pblk = lambda j: pl.BlockSpec((None, tm, D_MODEL), lambda b, i: (b, i, j))
    prev = lambda j: pl.BlockSpec((None, P_HALO, D_MODEL), lambda b, i: (b, jnp.maximum(i * rh - 1, 0), j))
    nxt = lambda j: pl.BlockSpec((None, P_HALO, D_MODEL),
                                 lambda b, i: (b, jnp.minimum((i + 1) * rh, s // P_HALO - 1), j))
    const = lambda shape: pl.BlockSpec(shape, lambda b, i: (0,) * len(shape))
    wspec = const((D_MODEL, D_MODEL))
    return pl.pallas_call(
        _merge_kernel,
        grid=(bn, s // tm),
        in_specs=[tok] * 5 + [pblk(j) for j in (BLK_MO, BLK_DZ, BLK_CB, BLK_CC, BLK_CX, BLK_GA, BLK_GB, BLK_GC)]
        + [prev(BLK_CC), prev(BLK_CX), nxt(BLK_CC), nxt(BLK_CX)]
        + [pl.BlockSpec((None, 6, D_MODEL), lambda b, i: (b, 0, 0))]
        + [const((1, D_MODEL)), const((1, D_MODEL)), const((3, D_MODEL))] + [wspec] * 4,
        out_specs=tok,
        out_shape=jax.ShapeDtypeStruct((bn, s, D_MODEL), F32),
        compiler_params=_cparams(("arbitrary", "arbitrary")),
        name="merge",
    )(x, hf, hb, of, ob, p, p, p, p, p, p, p, p, p, p, p, p, mod, m_norm_g, dn_norm_g, cm_conv_w,
      w_br_m, w_br_d, w_br_c, w_out)


FF_TILE = 256


def _ffn_kernel(x_ref, xp_ref, xn_ref, mod_ref, g_ref, wu_ref, cw_ref, wd_ref, fg_ref, o_ref, acc_ref, *, final):
    i = pl.program_id(1)
    nblk = pl.num_programs(1)
    x = x_ref[...]
    tm = x.shape[0]
    xa = jnp.concatenate([xp_ref[...], x, xn_ref[...]], axis=0)
    hf = _norm_mod(xa, g_ref[...], mod_ref[4:5, :], mod_ref[3:4, :])
    h = hf.astype(BF16)
    hmid = hf[8:8 + tm].astype(BF16)
    rows = tm + 16
    row = lax.broadcasted_iota(jnp.int32, (rows, FF_TILE), 0)
    kill_prev = jnp.logical_and(row == 7, i == 0)
    kill_next = jnp.logical_and(row == tm + 8, i == nblk - 1)
    kill = jnp.logical_or(kill_prev, kill_next)
    for c in range(D_FF // FF_TILE):
        cs = slice(c * FF_TILE, (c + 1) * FF_TILE)
        a = _dot(h, wu_ref[:, cs])
        a = jnp.where(kill, 0.0, a)
        b = _dot(hmid, wu_ref[:, D_FF + c * FF_TILE:D_FF + (c + 1) * FF_TILE])
        conv = (pltpu.roll(a, 1, 0) * cw_ref[0:1, cs] + a * cw_ref[1:2, cs]
                + pltpu.roll(a, rows - 1, 0) * cw_ref[2:3, cs])[8:8 + tm]
        y = (_silu(conv) * b).astype(BF16)
        contrib = _dot(y, wd_ref[cs, :])
        if c == 0:
            acc_ref[...] = contrib
        else:
            acc_ref[...] += contrib
    out = x + mod_ref[5:6, :] * acc_ref[...]
    if final:
        out = (out * lax.rsqrt(jnp.mean(out * out, axis=-1, keepdims=True) + EPS)) * fg_ref[...]
    o_ref[...] = out


def _ffn(x, mod, norm_g, w_up, conv_w, w_down, final_g, final):
    bn, s, _ = x.shape
    tm = min(s, 512)
    r8 = tm // 8
    const = lambda shape: pl.BlockSpec(shape, lambda b, i: (0,) * len(shape))
    return pl.pallas_call(
        functools.partial(_ffn_kernel, final=final),
        grid=(bn, s // tm),
        in_specs=[
            pl.BlockSpec((None, tm, D_MODEL), lambda b, i: (b, i, 0)),
            pl.BlockSpec((None, 8, D_MODEL), lambda b, i: (b, jnp.maximum(i * r8 - 1, 0), 0)),
            pl.BlockSpec((None, 8, D_MODEL), lambda b, i: (b, jnp.minimum((i + 1) * r8, s // 8 - 1), 0)),
            pl.BlockSpec((None, 6, D_MODEL), lambda b, i: (b, 0, 0)),
            const((1, D_MODEL)), const((D_MODEL, 2 * D_FF)), const((3, D_FF)), const((D_FF, D_MODEL)),
            const((1, D_MODEL)),
        ],
        out_specs=pl.BlockSpec((None, tm, D_MODEL), lambda b, i: (b, i, 0)),
        out_shape=jax.ShapeDtypeStruct((bn, s, D_MODEL), F32),
        scratch_shapes=[pltpu.VMEM((tm, D_MODEL), F32)],
        compiler_params=_cparams(("arbitrary", "arbitrary")),
        name="ffn_final" if final else "ffn",
    )(x, x, x, mod, norm_g, w_up, conv_w, w_down, final_g)


def _prepare(w_in, m_gate_b, m_norm_g, dn_a_log, dn_dt_bias, dn_norm_g):
    bounds = [0]
    for cw in COLS:
        bounds.append(bounds[-1] + cw)
    seg = [w_in[:, :, bounds[j]:bounds[j + 1]] for j in range(len(COLS))]
    m_q, m_k, m_v, m_o, m_if, d_qkv, d_z, d_ba, c_bcx, br_g = seg
    w_main = jnp.concatenate([m_q * (M_DQK ** -0.5), m_k, m_v, m_o, d_qkv, d_z, c_bcx, br_g], axis=-1).astype(BF16)
    pad = jnp.zeros(w_in.shape[:2] + (LANE - 4 * M_HEADS - 4 * D_HEADS,), w_in.dtype)
    w_gate = jnp.concatenate([m_if, d_ba, pad], axis=-1).astype(BF16)
    w_gate_t = jnp.swapaxes(w_gate, 1, 2)

    lanes = lambda t, off: jnp.pad(t, ((0, 0), (off, LANE - off - t.shape[1])))
    bias = lanes(m_gate_b, GATE_M)
    neg_a = -jnp.exp(dn_a_log.astype(F32))
    zeros_h = jnp.zeros((DEPTH, D_HEADS), F32)
    a_l = lanes(jnp.concatenate([zeros_h, neg_a[:, 0], zeros_h, neg_a[:, 1]], axis=-1), GATE_D)
    dtb = dn_dt_bias.astype(F32)
    dt_l = lanes(jnp.concatenate([zeros_h, dtb[:, 0], zeros_h, dtb[:, 1]], axis=-1), GATE_D)
    as_row = lambda t: t[:, None, :]
    as_col = lambda t: t[:, :, None]
    return dict(
        w_main=w_main, w_gate=w_gate, w_gate_t=w_gate_t,
        bias_r=as_row(bias), bias_c=as_col(bias),
        a_r=as_row(a_l), a_c=as_col(a_l), dt_r=as_row(dt_l), dt_c=as_col(dt_l),
        m_norm=jnp.tile(m_norm_g, (1, M_HEADS))[:, None, :], dn_norm=jnp.tile(dn_norm_g, (1, D_HEADS))[:, None, :],
    )


def _trunk(x, c, prm, norm1_g, w_ada, b_ada, dn_conv_w, cm_conv_w, w_br_m, w_br_d, w_br_c, w_out, norm2_g,
           w_up, ffn_conv_w, w_down, final_g):
    mod = _modulation(c, w_ada, b_ada)
    fg = final_g[None, :]
    for l in range(DEPTH):
        p, g, gt = _proj_in(x, mod[l], norm1_g[l][None, :], prm["w_main"][l], prm["w_gate"][l], prm["w_gate_t"][l])
        hf, hb = _mlstm(p, g, gt, prm["bias_r"][l], prm["bias_c"][l])
        prep = _gdn_prep(p, g, gt, dn_conv_w[l], prm["a_r"][l], prm["a_c"][l], prm["dt_r"][l], prm["dt_c"][l])
        of, ob = _gdn_scan(prep)
        x = _merge(x, hf, hb, of, ob, p, mod[l], prm["m_norm"][l], prm["dn_norm"][l], cm_conv_w[l],
                   w_br_m[l], w_br_d[l], w_br_c[l], w_out[l])
        x = _ffn(x, mod[l], norm2_g[l][None, :], w_up[l], ffn_conv_w[l], w_down[l], fg, l == DEPTH - 1)
    return x


def kernel(x_prompt, x_sample, c_prompt, c_sample, norm1_g, w_ada, b_ada, w_in, m_gate_b, m_norm_g, dn_conv_w, dn_a_log, dn_dt_bias, dn_norm_g, cm_conv_w, w_br_m, w_br_d, w_br_c, w_out, norm2_g, w_up, ffn_conv_w, w_down, final_g):
    prm = _prepare(w_in, m_gate_b, m_norm_g, dn_a_log, dn_dt_bias, dn_norm_g)
    bf = lambda t: t.astype(BF16)
    args = (prm, norm1_g, w_ada, b_ada, dn_conv_w, cm_conv_w, bf(w_br_m), bf(w_br_d), bf(w_br_c), bf(w_out),
            norm2_g, bf(w_up), ffn_conv_w, bf(w_down), final_g)
    return (_trunk(x_prompt, c_prompt, *args), _trunk(x_sample, c_sample, *args))
```

```python
import functools

import jax
import jax.numpy as jnp
from jax import lax
from jax.experimental import pallas as pl
from jax.experimental.pallas import tpu as pltpu

F32 = jnp.float32
BF16 = jnp.bfloat16

D_MODEL = 1024
DEPTH = 4
M_HEADS = 4
M_DQK = 128
M_DV = 256
M_CHUNK = 128
D_HEADS = 8
D_DK = 128
D_DV = 128
G_CHUNK = 128
D_FF = 2816
EPS = 1e-6

M_QK = M_HEADS * M_DQK
M_V = M_HEADS * M_DV
D_QK = D_HEADS * D_DK
D_V = D_HEADS * D_DV
COLS = (M_QK, M_QK, M_V, M_V, 4 * M_HEADS, 2 * D_QK + D_V, D_V, 4 * D_HEADS, 3 * D_MODEL, 3 * D_MODEL)

LANE = 128
P_MAIN = 13 * 1024
BLK_MO, BLK_DZ, BLK_CB, BLK_CC, BLK_CX, BLK_GA, BLK_GB, BLK_GC = 2, 6, 7, 8, 9, 10, 11, 12
GATE_M = 0
GATE_D = 4 * M_HEADS

M_BATCH = 2
P_DTYPE = BF16
P_HALO = 16
VMEM_LIMIT = 56 * 1024 * 1024


def _cparams(sem):
    return pltpu.CompilerParams(dimension_semantics=sem, vmem_limit_bytes=VMEM_LIMIT)


def _sigmoid(x):
    return 0.5 * jnp.tanh(0.5 * x) + 0.5


def _silu(x):
    return x * _sigmoid(x)


def _softplus(x):
    return jnp.maximum(x, 0.0) + jnp.log(1.0 + jnp.exp(-jnp.abs(x)))


def _dot(a, b):
    return jnp.dot(a, b, preferred_element_type=F32)


def _dot_nt(a, b):
    return lax.dot_general(a, b, (((1,), (1,)), ((), ())), preferred_element_type=F32)


def _dot_tn(a, b):
    return lax.dot_general(a, b, (((0,), (0,)), ((), ())), preferred_element_type=F32)


def _dot_hi(a, b):
    return jnp.dot(a, b, preferred_element_type=F32, precision=lax.Precision.HIGHEST)


def _split3(x):
    h1 = x.astype(BF16)
    r1 = x - h1.astype(F32)
    h2 = r1.astype(BF16)
    h3 = (r1 - h2.astype(F32)).astype(BF16)
    return h1, h2, h3


def _cumsum_cols(mask01, x):
    h1, h2, h3 = _split3(x)
    return _dot(mask01, h1) + (_dot(mask01, h2) + _dot(mask01, h3))


def _cumsum_rows(x, mask01):
    h1, h2, h3 = _split3(x)
    return _dot(h1, mask01) + (_dot(h2, mask01) + _dot(h3, mask01))


SUBLANES = 8


def _shift_rows(x, prev_row, next_row):
    rows = x.shape[0]
    row = lax.broadcasted_iota(jnp.int32, (SUBLANES, x.shape[1]), 0)
    down = pltpu.roll(x, 1, 0)
    up = pltpu.roll(x, rows - 1, 0)
    down = jnp.concatenate([jnp.where(row == 0, prev_row, down[:SUBLANES]), down[SUBLANES:]], axis=0)
    up = jnp.concatenate([up[:rows - SUBLANES], jnp.where(row == SUBLANES - 1, next_row, up[rows - SUBLANES:])],
                         axis=0)
    return down, up


def _scan_masks(n, reverse):
    r = lax.broadcasted_iota(jnp.int32, (n, n), 0)
    c = lax.broadcasted_iota(jnp.int32, (n, n), 1)
    if reverse:
        return c >= r, c > r, r >= c
    return c <= r, c < r, r <= c


def _mod_kernel(c_ref, w_ref, b_ref, o_ref):
    o_ref[...] = _dot_hi(_silu(c_ref[...]), w_ref[...]) + b_ref[...]


def _modulation(c, w_ada, b_ada):
    bn = c.shape[0]
    out = pl.pallas_call(
        _mod_kernel,
        grid=(DEPTH, 6),
        in_specs=[
            pl.BlockSpec((bn, D_MODEL), lambda l, j: (0, 0)),
            pl.BlockSpec((None, D_MODEL, D_MODEL), lambda l, j: (l, 0, j)),
            pl.BlockSpec((None, 1, D_MODEL), lambda l, j: (l, 0, j)),
        ],
        out_specs=pl.BlockSpec((None, bn, D_MODEL), lambda l, j: (l, 0, j)),
        out_shape=jax.ShapeDtypeStruct((DEPTH, bn, 6 * D_MODEL), F32),
        compiler_params=_cparams(("arbitrary", "arbitrary")),
        name="adaln_mod",
    )(c, w_ada, b_ada.reshape(DEPTH, 1, 6 * D_MODEL))
    return out.reshape(DEPTH, bn, 6, D_MODEL)


def _norm_mod(x, g, scale, shift):
    y = x * lax.rsqrt(jnp.mean(x * x, axis=-1, keepdims=True) + EPS)
    return (y * g) * (1.0 + scale) + shift


PROJ_SUB = 256


def _proj_kernel(x_ref, mod_ref, g_ref, w_ref, wg_ref, wgt_ref, o_ref, og_ref, ogt_ref, h_ref):
    @pl.when(pl.program_id(2) == 0)
    def _():
        h = _norm_mod(x_ref[...], g_ref[...], mod_ref[1:2, :], mod_ref[0:1, :]).astype(BF16)
        h_ref[...] = h
        og_ref[...] = _dot(h, wg_ref[...])
        ogt_ref[...] = _dot_nt(wgt_ref[...], h)

    h = h_ref[...]
    for c in range(o_ref.shape[1] // PROJ_SUB):
        cs = slice(c * PROJ_SUB, (c + 1) * PROJ_SUB)
        o_ref[:, cs] = _dot(h, w_ref[:, cs]).astype(o_ref.dtype)


def _proj_in(x, mod, norm_g, w_main, w_gate, w_gate_t):
    bn, s, _ = x.shape
    tm = min(s, 1024)
    tn = P_MAIN // 4
    return pl.pallas_call(
        _proj_kernel,
        grid=(bn, s // tm, P_MAIN // tn),
        in_specs=[
            pl.BlockSpec((None, tm, D_MODEL), lambda b, i, j: (b, i, 0)),
            pl.BlockSpec((None, 6, D_MODEL), lambda b, i, j: (b, 0, 0)),
            pl.BlockSpec((1, D_MODEL), lambda b, i, j: (0, 0)),
            pl.BlockSpec((D_MODEL, tn), lambda b, i, j: (0, j)),
            pl.BlockSpec((D_MODEL, LANE), lambda b, i, j: (0, 0)),
            pl.BlockSpec((LANE, D_MODEL), lambda b, i, j: (0, 0)),
        ],
        out_specs=[
            pl.BlockSpec((None, tm, tn), lambda b, i, j: (b, i, j)),
            pl.BlockSpec((None, tm, LANE), lambda b, i, j: (b, i, 0)),
            pl.BlockSpec((None, LANE, tm), lambda b, i, j: (b, 0, i)),
        ],
        out_shape=[
            jax.ShapeDtypeStruct((bn, s, P_MAIN), P_DTYPE),
            jax.ShapeDtypeStruct((bn, s, LANE), F32),
            jax.ShapeDtypeStruct((bn, LANE, s), F32),
        ],
        scratch_shapes=[pltpu.VMEM((tm, D_MODEL), BF16)],
        compiler_params=_cparams(("arbitrary", "arbitrary", "arbitrary")),
        name="proj_in",
    )(x, mod, norm_g, w_main, w_gate, w_gate_t)


def _log_sigmoid(x):
    return jnp.minimum(x, 0.0) - jnp.log(1.0 + jnp.exp(-jnp.abs(x)))


def _mlstm_kernel(qf_ref, kf_ref, vf_ref, gf_ref, gtf_ref, qb_ref, kb_ref, vb_ref, gb_ref, gtb_ref,
                  bias_r_ref, bias_c_ref, hf_ref, hb_ref, c_ref, m_ref):
    @pl.when(pl.program_id(1) == 0)
    def _():
        c_ref[...] = jnp.zeros_like(c_ref)
        m_ref[...] = jnp.zeros_like(m_ref)


    L = M_CHUNK
    dirs = (
        (qf_ref, kf_ref, vf_ref, gf_ref, gtf_ref, hf_ref, False, L - 1, 0),
        (qb_ref, kb_ref, vb_ref, gb_ref, gtb_ref, hb_ref, True, 0, 2),
    )
    units = []
    for bi, d in [(bi, d) for bi in range(M_BATCH) for d in range(2)]:
        q_ref, k_ref, v_ref, g_ref, gt_ref, h_ref, reverse, last, gbase = dirs[d]
        mask, _, mask_t = _scan_masks(L, reverse)
        gc = g_ref[bi] + bias_r_ref[...]
        gr = gt_ref[bi] + bias_c_ref[...]
        bc_all = _cumsum_cols(mask.astype(BF16), _log_sigmoid(gc))
        br_all = _cumsum_rows(_log_sigmoid(gr), mask_t.astype(BF16))
        for h in range(M_HEADS):
            ci = GATE_M + gbase * M_HEADS + h
            cf = ci + M_HEADS
            idx = (bi * 2 + d) * M_HEADS + h
            k = k_ref[bi, :, h * M_DQK:(h + 1) * M_DQK].astype(F32)
            vb = jnp.concatenate([v_ref[bi, :, h * M_DV:(h + 1) * M_DV].astype(BF16), jnp.ones((L, LANE), BF16)],
                                 axis=1)
            units.append(dict(
                idx=idx, mask=mask, last=last, h_ref=h_ref, bi=bi, h=h,
                b_col=bc_all[:, cf:cf + 1], b_row=br_all[cf:cf + 1, :], i_col=gc[:, ci:ci + 1], i_row=gr[ci:ci + 1, :],
                m_old=m_ref[idx:idx + 1, 0:1], c_old=c_ref[idx],
                k=k, qb=q_ref[bi, :, h * M_DQK:(h + 1) * M_DQK].astype(BF16), vb=vb))

    for u in units:
        u["qk"] = _dot_nt(u["qb"], u["k"].astype(BF16))
        u["qc"] = _dot(u["qb"], u["c_old"].astype(BF16))
    for u in units:
        u["dm"] = jnp.where(u["mask"], u["b_col"] - u["b_row"] + u["i_row"], -jnp.inf)
    for u in units:
        u["rowmax"] = jnp.max(u["dm"], axis=-1, keepdims=True)
    for u in units:
        a_col = u["b_col"] + u["m_old"]
        u["mt"] = jnp.maximum(a_col, u["rowmax"])
        u["sa"] = jnp.exp(a_col - u["mt"])
    for u in units:
        u["floor"] = jnp.exp(-jnp.broadcast_to(u["mt"], (L, LANE)))
        u["s"] = u["qk"] * jnp.exp(u["dm"] - u["mt"])
    for u in units:
        u["sv"] = _dot(u["s"].astype(BF16), u["vb"])
    for u in units:
        u["num"] = u["sa"] * u["qc"] + u["sv"]
    for u in units:
        num = u["num"]
        inv = 1.0 / jnp.maximum(jnp.abs(num[:, M_DV:]), u["floor"])
        for t in range(M_DV // LANE):
            lo = u["h"] * M_DV + t * LANE
            u["h_ref"][u["bi"], :, lo:lo + LANE] = num[:, t * LANE:(t + 1) * LANE] * inv
    for u in units:
        last = u["last"]
        b_tot = u["b_col"][last:last + 1, :]
        lw_row = b_tot - u["b_row"] + u["i_row"]
        u["lw_col"] = b_tot - u["b_col"] + u["i_col"]
        u["bm"] = b_tot + u["m_old"]
        u["lwmax"] = jnp.max(lw_row, axis=-1, keepdims=True)
    for u in units:
        u["m_new"] = jnp.maximum(u["bm"], u["lwmax"])
        u["wk"] = jnp.exp(u["lw_col"] - u["m_new"])
    for u in units:
        u["wkk"] = (u["wk"] * u["k"]).astype(BF16)
    for u in units:
        idx = u["idx"]
        dec = jnp.exp(u["bm"] - u["m_new"])
        c_ref[idx] = dec * u["c_old"] + _dot_tn(u["wkk"], u["vb"])
        m_ref[idx:idx + 1, :] = jnp.broadcast_to(u["m_new"], (1, LANE))


def _mlstm(p, g, gt, bias_r, bias_c):
    bn, s, _ = p.shape
    L = M_CHUNK
    n = s // L
    mb = M_BATCH
    blk = lambda w, j, rev: pl.BlockSpec((mb, L, w), (lambda b, i: (b, n - 1 - i, j)) if rev else (lambda b, i: (b, i, j)))
    gtblk = lambda rev: pl.BlockSpec((mb, LANE, L), (lambda b, i: (b, 0, n - 1 - i)) if rev else (lambda b, i: (b, 0, i)))
    in_specs = []
    for rev in (False, True):
        in_specs += [blk(M_QK, 0, rev), blk(M_QK, 1, rev), blk(M_V, 1, rev), blk(LANE, 0, rev), gtblk(rev)]
    in_specs += [pl.BlockSpec((1, LANE), lambda b, i: (0, 0)), pl.BlockSpec((LANE, 1), lambda b, i: (0, 0))]
    n_units = mb * 2 * M_HEADS
    return pl.pallas_call(
        _mlstm_kernel,
        grid=(bn // mb, n),
        in_specs=in_specs,
        out_specs=[blk(M_V, 0, False), blk(M_V, 0, True)],
        out_shape=[jax.ShapeDtypeStruct((bn, s, M_V), F32)] * 2,
        scratch_shapes=[
            pltpu.VMEM((n_units, M_DQK, M_DV + LANE), F32),
            pltpu.VMEM((n_units, LANE), F32),
        ],
        compiler_params=_cparams(("arbitrary", "arbitrary")),
        name="mlstm",
    )(p, p, p, g, gt, p, p, p, g, gt, bias_r, bias_c)


G_SCAN_BATCH = 2
G_PREP_CHUNKS = 2
INV_BASE = 8


def _lane_pair(a, b):
    return jnp.concatenate([a, b], axis=1)


def _blockdiag(pair):
    n = pair.shape[0]
    z = jnp.zeros((n, n), pair.dtype)
    return jnp.concatenate([_lane_pair(pair[:, :n], z), _lane_pair(z, pair[:, n:])], axis=0)


def _pair_dot(lhs, rhs):
    return _dot(lhs, _blockdiag(rhs))


def _unit_tri_inverse_pairs(a_list):
    n = a_list[0].shape[0]
    r = lax.broadcasted_iota(jnp.int32, (n, 2 * n), 0)
    c = lax.broadcasted_iota(jnp.int32, (n, 2 * n), 1) % n
    same_block = lambda size: (r // size) == (c // size)
    eye = (r == c).astype(F32)
    bf = lambda xs: [x.astype(BF16) for x in xs]
    a0 = [jnp.where(same_block(INV_BASE), a, 0.0) for a in a_list]
    a0b = bf(a0)
    a2 = [_pair_dot(x, x) for x in a0b]
    a2b = bf(a2)
    a3 = [_pair_dot(x2, x) for x2, x in zip(a2b, a0b)]
    a4 = [_pair_dot(x2, x2) for x2 in a2b]
    p = [eye - x + x2 - x3 for x, x2, x3 in zip(a0, a2, a3)]
    t = [pp + _pair_dot(pb, x4) for pp, pb, x4 in zip(p, bf(p), bf(a4))]
    ab = bf(a_list)
    size = INV_BASE
    while size < n:
        level = jnp.logical_and(same_block(2 * size), jnp.logical_not(same_block(size))).astype(BF16)
        tb = bf(t)
        x = [_pair_dot(a * level, tt) for a, tt in zip(ab, tb)]
        t = [tt - _pair_dot(tq, xx) for tt, tq, xx in zip(t, tb, bf(x))]
        size *= 2
    return t


def _gdn_prep_kernel(x_ref, xp_ref, xn_ref, g_ref, gt_ref, cw_ref, ar_ref, ac_ref, dr_ref, dc_ref,
                     v_ref, kgf_ref, qgf_ref, kdf_ref, tf_ref, qkf_ref, egf_ref,
                     kgb_ref, qgb_ref, kdb_ref, tb_ref, qkb_ref, egb_ref):
    i = pl.program_id(1)
    nblk = pl.num_programs(1)
    L = G_CHUNK
    rows = x_ref.shape[0]
    x = x_ref[...].astype(F32)
    prev_row = jnp.where(i > 0, xp_ref[P_HALO - 1:P_HALO, :].astype(F32), 0.0)
    next_row = jnp.where(i < nblk - 1, xn_ref[0:1, :].astype(F32), 0.0)
    xm, xq = _shift_rows(x, prev_row, next_row)
    y = _silu(xm * cw_ref[0:1, :] + x * cw_ref[1:2, :] + xq * cw_ref[2:3, :])
    v_ref[...] = y[:, 2 * D_QK:].astype(v_ref.dtype)

    gc_all = g_ref[...]
    gr_all = gt_ref[...]
    beta_c_all = _sigmoid(gc_all)
    beta_r_all = _sigmoid(gr_all)
    lg_c_all = ar_ref[...] * _softplus(gc_all + dr_ref[...])
    lg_r_all = ac_ref[...] * _softplus(gr_all + dc_ref[...])

    units = []
    amats = []
    for ck in range(rows // L):
        rs = slice(ck * L, (ck + 1) * L)
        beta_c, beta_r, lg_c, lg_r = beta_c_all[rs], beta_r_all[:, rs], lg_c_all[rs], lg_r_all[:, rs]
        qn, kn = [], []
        for h in range(D_HEADS):
            qh = y[rs, h * D_DK:(h + 1) * D_DK]
            kh = y[rs, D_QK + h * D_DK:D_QK + (h + 1) * D_DK]
            qn.append(qh * lax.rsqrt(jnp.sum(qh * qh, axis=-1, keepdims=True) + EPS) * (D_DK ** -0.5))
            kn.append(kh * lax.rsqrt(jnp.sum(kh * kh, axis=-1, keepdims=True) + EPS))
        kk_p, qk_p = [], []
        for hp in range(D_HEADS // 2):
            kp = _lane_pair(kn[2 * hp], kn[2 * hp + 1]).astype(BF16)
            qp = _lane_pair(qn[2 * hp], qn[2 * hp + 1]).astype(BF16)
            kbd = _blockdiag(kp)
            kk_p.append(_dot_nt(kp, kbd))
            qk_p.append(_dot_nt(qp, kbd))

        for reverse, last, gbase, refs in (
            (False, L - 1, 0, (kgf_ref, qgf_ref, kdf_ref, tf_ref, qkf_ref, egf_ref)),
            (True, 0, 2, (kgb_ref, qgb_ref, kdb_ref, tb_ref, qkb_ref, egb_ref)),
        ):
            mask, strict, mask_t = _scan_masks(L, reverse)
            cum_c = _cumsum_cols(mask.astype(BF16), lg_c)
            cum_r = _cumsum_rows(lg_r, mask_t.astype(BF16))
            kg_ref, qg_ref, kd_ref, t_ref, qkd_ref, eg_ref = refs
            for hp in range(D_HEADS // 2):
                decays, a_halves, b_rows = [], [], []
                for half, h in enumerate((2 * hp, 2 * hp + 1)):
                    cb = GATE_D + gbase * D_HEADS + h
                    ca = cb + D_HEADS
                    gcum_c = cum_c[:, ca:ca + 1]
                    decay = jnp.exp(jnp.where(mask, gcum_c - cum_r[ca:ca + 1, :], -jnp.inf))
                    kk = kk_p[hp][:, half * L:(half + 1) * L]
                    a_halves.append(jnp.where(strict, (beta_c[:, cb:cb + 1] * kk) * decay, 0.0))
                    decays.append(decay)
                    b_rows.append(beta_r[cb:cb + 1, :])
                    g_tot = gcum_c[last:last + 1, :]
                    eg = jnp.exp(gcum_c)
                    hs = slice(h * D_DK, (h + 1) * D_DK)
                    kg_ref[rs, hs] = (kn[h] * eg).astype(kg_ref.dtype)
                    qg_ref[rs, hs] = (qn[h] * eg).astype(qg_ref.dtype)
                    kd_ref[rs, hs] = (kn[h] * jnp.exp(g_tot - gcum_c)).astype(kd_ref.dtype)
                    er = ck * D_HEADS + h
                    eg_ref[er:er + 1, :] = jnp.broadcast_to(jnp.exp(g_tot), (1, LANE))
                ps = slice(hp * 2 * L, (hp + 1) * 2 * L)
                qkd_ref[rs, ps] = (qk_p[hp] * _lane_pair(*decays)).astype(qkd_ref.dtype)
                amats.append(_lane_pair(*a_halves))
                units.append((t_ref, rs, ps, _lane_pair(*b_rows)))

    for (t_ref, rs, ps, b_row), tmat in zip(units, _unit_tri_inverse_pairs(amats)):
        t_ref[rs, ps] = (tmat * b_row).astype(t_ref.dtype)


def _gdn_prep(p, g, gt, conv_w, a_row, a_col, dt_row, dt_col):
    bn, s, _ = p.shape
    L = G_CHUNK
    n = s // L
    w3 = 2 * D_QK + D_V
    nck = min(G_PREP_CHUNKS, n)
    rows = nck * L
    rh = rows // P_HALO
    tok = lambda w: pl.BlockSpec((None, rows, w), lambda b, i: (b, i, 0))
    const = lambda shape: pl.BlockSpec(shape, lambda b, i: (0,) * len(shape))
    big = jax.ShapeDtypeStruct((bn, s, D_QK), BF16)
    egs = jax.ShapeDtypeStruct((bn, n * D_HEADS, LANE), F32)
    per_dir_specs = [tok(D_QK)] * 5 + [pl.BlockSpec((None, nck * D_HEADS, LANE), lambda b, i: (b, i, 0))]
    per_dir_shapes = [big] * 5 + [egs]
    return pl.pallas_call(
        _gdn_prep_kernel,
        grid=(bn, n // nck),
        in_specs=[
            pl.BlockSpec((None, rows, w3), lambda b, i: (b, i, 1)),
            pl.BlockSpec((None, P_HALO, w3), lambda b, i: (b, jnp.maximum(i * rh - 1, 0), 1)),
            pl.BlockSpec((None, P_HALO, w3), lambda b, i: (b, jnp.minimum((i + 1) * rh, s // P_HALO - 1), 1)),
            pl.BlockSpec((None, rows, LANE), lambda b, i: (b, i, 0)),
            pl.BlockSpec((None, LANE, rows), lambda b, i: (b, 0, i)),
            const((3, w3)), const((1, LANE)), const((LANE, 1)), const((1, LANE)), const((LANE, 1)),
        ],
        out_specs=[tok(D_V)] + per_dir_specs * 2,
        out_shape=[big] + per_dir_shapes * 2,
        compiler_params=_cparams(("arbitrary", "arbitrary")),
        name="gdn_prep",
    )(p, p, p, g, gt, conv_w, a_row, a_col, dt_row, dt_col)


def _gdn_scan_kernel(vf_ref, kgf_ref, qgf_ref, kdf_ref, tf_ref, qkf_ref, egf_ref,
                     vb_ref, kgb_ref, qgb_ref, kdb_ref, tb_ref, qkb_ref, egb_ref,
                     of_ref, ob_ref, s_ref):
    @pl.when(pl.program_id(1) == 0)
    def _():
        s_ref[...] = jnp.zeros_like(s_ref)

    L = G_CHUNK
    units = []
    for bi in range(vf_ref.shape[0]):
        for d, refs in enumerate((
            (vf_ref, kgf_ref, qgf_ref, kdf_ref, tf_ref, qkf_ref, egf_ref, of_ref),
            (vb_ref, kgb_ref, qgb_ref, kdb_ref, tb_ref, qkb_ref, egb_ref, ob_ref),
        )):
            for hp in range(D_HEADS // 2):
                idx = (bi * 2 + d) * D_HEADS + 2 * hp
                units.append((bi, idx, 2 * hp, slice(hp * 2 * L, (hp + 1) * 2 * L)) + refs)

    xs = []
    for bi, idx, h0, ps, v_ref, kg_ref, qg_ref, kd_ref, t_ref, qkd_ref, eg_ref, o_ref in units:
        s_bd = _blockdiag(_lane_pair(s_ref[idx].astype(BF16), s_ref[idx + 1].astype(BF16)))
        xs.append(_dot(jnp.concatenate([kg_ref[bi, :, ps], qg_ref[bi, :, ps]], axis=0), s_bd))
    vns = []
    for x, (bi, idx, h0, ps, v_ref, kg_ref, qg_ref, kd_ref, t_ref, qkd_ref, eg_ref, o_ref) in zip(xs, units):
        rhs = (v_ref[bi, :, ps].astype(F32) - x[:L]).astype(BF16)
        vns.append(_pair_dot(t_ref[bi, :, ps], rhs).astype(BF16))
    for x, vn, (bi, idx, h0, ps, v_ref, kg_ref, qg_ref, kd_ref, t_ref, qkd_ref, eg_ref, o_ref) in zip(xs, vns, units):
        o_ref[bi, :, ps] = x[L:] + _pair_dot(qkd_ref[bi, :, ps], vn)
        for half in range(2):
            hs = slice((h0 + half) * D_DK, (h0 + half + 1) * D_DK)
            s_ref[idx + half] = (eg_ref[bi, h0 + half:h0 + half + 1, :] * s_ref[idx + half]
                                 + _dot_tn(kd_ref[bi, :, hs], vn[:, half * L:(half + 1) * L]))


def _gdn_scan(prep):
    v = prep[0]
    bn, s, _ = v.shape
    L = G_CHUNK
    n = s // L

    sb = G_SCAN_BATCH

    def specs(rev):
        im = (lambda b, i: (b, n - 1 - i, 0)) if rev else (lambda b, i: (b, i, 0))
        return [pl.BlockSpec((sb, L, D_QK), im)] * 6 + [pl.BlockSpec((sb, D_HEADS, LANE), im)]

    fwd_ops = [v] + list(prep[1:7])
    bwd_ops = [v] + list(prep[7:13])
    return pl.pallas_call(
        _gdn_scan_kernel,
        grid=(bn // sb, n),
        in_specs=specs(False) + specs(True),
        out_specs=[pl.BlockSpec((sb, L, D_V), lambda b, i: (b, i, 0)),
                   pl.BlockSpec((sb, L, D_V), lambda b, i: (b, n - 1 - i, 0))],
        out_shape=[jax.ShapeDtypeStruct((bn, s, D_V), F32)] * 2,
        scratch_shapes=[pltpu.VMEM((sb * 2 * D_HEADS, D_DK, D_DV), F32)],
        compiler_params=_cparams(("arbitrary", "arbitrary")),
        name="gdn_scan",
    )(*fwd_ops, *bwd_ops)


def _head_rmsnorm(x, g, width):
    outs = []
    for h in range(x.shape[1] // width):
        xs = x[:, h * width:(h + 1) * width]
        outs.append(xs * lax.rsqrt(jnp.mean(xs * xs, axis=-1, keepdims=True) + EPS))
    return jnp.concatenate(outs, axis=1) * g


def _merge_kernel(x_ref, hf_ref, hb_ref, of_ref, ob_ref, mo_ref, dz_ref, cb_ref, cc_ref, cx_ref,
                  ga_ref, gb_ref, gc_ref, ccp_ref, cxp_ref, ccn_ref, cxn_ref, mod_ref,
                  mg_ref, dg_ref, cw_ref, wm_ref, wd_ref, wc_ref, wo_ref, o_ref):
    i = pl.program_id(1)
    nblk = pl.num_programs(1)
    f = lambda ref: ref[...].astype(F32)
    hm = _head_rmsnorm(hf_ref[...] + hb_ref[...], mg_ref[...], M_DV)
    out_a = _dot(_sigmoid(mo_ref[...].astype(BF16)) * hm.astype(BF16), wm_ref[...])
    od = _head_rmsnorm(of_ref[...] + ob_ref[...], dg_ref[...], D_DV)
    out_b = _dot(od.astype(BF16) * _silu(dz_ref[...].astype(BF16)), wd_ref[...])

    u = f(cc_ref) * f(cx_ref)
    prev_row = jnp.where(i > 0, f(ccp_ref)[P_HALO - 1:P_HALO, :] * f(cxp_ref)[P_HALO - 1:P_HALO, :], 0.0)
    next_row = jnp.where(i < nblk - 1, f(ccn_ref)[0:1, :] * f(cxn_ref)[0:1, :], 0.0)
    um, up = _shift_rows(u, prev_row, next_row)
    conv = um * cw_ref[0:1, :] + u * cw_ref[1:2, :] + up * cw_ref[2:3, :]
    out_c = _dot(cb_ref[...].astype(BF16) * conv.astype(BF16), wc_ref[...])

    gate = lambda ref, out: _sigmoid(ref[...].astype(BF16)) * out.astype(BF16)
    merged = gate(ga_ref, out_a) + gate(gb_ref, out_b) + gate(gc_ref, out_c)
    o_ref[...] = x_ref[...] + mod_ref[2:3, :] * _dot(merged, wo_ref[...])


def _merge(x, hf, hb, of, ob, p, mod, m_norm_g, dn_norm_g, cm_conv_w, w_br_m, w_br_d, w_br_c, w_out):
    bn, s, _ = x.shape
    tm = min(s, 256)
    rh = tm // P_HALO
    tok = pl.BlockSpec((None, tm, D_MODEL), lambda b, i: (b, i, 0))
    pblk = lambda j: pl.BlockSpec((None, tm, D_MODEL), lambda b, i: (b, i, j))
    prev = lambda j: pl.BlockSpec((None, P_HALO, D_MODEL), lambda b, i: (b, jnp.maximum(i * rh - 1, 0), j))
    nxt = lambda j: pl.BlockSpec((None, P_HALO, D_MODEL),
                                 lambda b, i: (b, jnp.minimum((i + 1) * rh, s // P_HALO - 1), j))
    const = lambda shape: pl.BlockSpec(shape, lambda b, i: (0,) * len(shape))
    wspec = const((D_MODEL, D_MODEL))
    return pl.pallas_call(
        _merge_kernel,
        grid=(bn, s // tm),
        in_specs=[tok] * 5 + [pblk(j) for j in (BLK_MO, BLK_DZ, BLK_CB, BLK_CC, BLK_CX, BLK_GA, BLK_GB, BLK_GC)]
        + [prev(BLK_CC), prev(BLK_CX), nxt(BLK_CC), nxt(BLK_CX)]
        + [pl.BlockSpec((None, 6, D_MODEL), lambda b, i: (b, 0, 0))]
        + [const((1, D_MODEL)), const((1, D_MODEL)), const((3, D_MODEL))] + [wspec] * 4,
        out_specs=tok,
        out_shape=jax.ShapeDtypeStruct((bn, s, D_MODEL), F32),
        compiler_params=_cparams(("arbitrary", "arbitrary")),
        name="merge",
    )(x, hf, hb, of, ob, p, p, p, p, p, p, p, p, p, p, p, p, mod, m_norm_g, dn_norm_g, cm_conv_w,
      w_br_m, w_br_d, w_br_c, w_out)


FF_TILE = 256


def _ffn_kernel(x_ref, xp_ref, xn_ref, mod_ref, g_ref, wu_ref, cw_ref, wd_ref, fg_ref, o_ref, acc_ref, *, final):
    i = pl.program_id(1)
    nblk = pl.num_programs(1)
    x = x_ref[...]
    tm = x.shape[0]
    xa = jnp.concatenate([xp_ref[...], x, xn_ref[...]], axis=0)
    hf = _norm_mod(xa, g_ref[...], mod_ref[4:5, :], mod_ref[3:4, :])
    h = hf.astype(BF16)
    hmid = hf[8:8 + tm].astype(BF16)
    rows = tm + 16
    row = lax.broadcasted_iota(jnp.int32, (rows, FF_TILE), 0)
    kill_prev = jnp.logical_and(row == 7, i == 0)
    kill_next = jnp.logical_and(row == tm + 8, i == nblk - 1)
    kill = jnp.logical_or(kill_prev, kill_next)
    for c in range(D_FF // FF_TILE):
        cs = slice(c * FF_TILE, (c + 1) * FF_TILE)
        a = _dot(h, wu_ref[:, cs])
        a = jnp.where(kill, 0.0, a)
        b = _dot(hmid, wu_ref[:, D_FF + c * FF_TILE:D_FF + (c + 1) * FF_TILE])
        conv = (pltpu.roll(a, 1, 0) * cw_ref[0:1, cs] + a * cw_ref[1:2, cs]
                + pltpu.roll(a, rows - 1, 0) * cw_ref[2:3, cs])[8:8 + tm]
        y = (_silu(conv) * b).astype(BF16)
        contrib = _dot(y, wd_ref[cs, :])
        if c == 0:
            acc_ref[...] = contrib
        else:
            acc_ref[...] += contrib
    out = x + mod_ref[5:6, :] * acc_ref[...]
    if final:
        out = (out * lax.rsqrt(jnp.mean(out * out, axis=-1, keepdims=True) + EPS)) * fg_ref[...]
    o_ref[...] = out


def _ffn(x, mod, norm_g, w_up, conv_w, w_down, final_g, final):
    bn, s, _ = x.shape
    tm = min(s, 512)
    r8 = tm // 8
    const = lambda shape: pl.BlockSpec(shape, lambda b, i: (0,) * len(shape))
    return pl.pallas_call(
        functools.partial(_ffn_kernel, final=final),
        grid=(bn, s // tm),
        in_specs=[
            pl.BlockSpec((None, tm, D_MODEL), lambda b, i: (b, i, 0)),
            pl.BlockSpec((None, 8, D_MODEL), lambda b, i: (b, jnp.maximum(i * r8 - 1, 0), 0)),
            pl.BlockSpec((None, 8, D_MODEL), lambda b, i: (b, jnp.minimum((i + 1) * r8, s // 8 - 1), 0)),
            pl.BlockSpec((None, 6, D_MODEL), lambda b, i: (b, 0, 0)),
            const((1, D_MODEL)), const((D_MODEL, 2 * D_FF)), const((3, D_FF)), const((D_FF, D_MODEL)),
            const((1, D_MODEL)),
        ],
        out_specs=pl.BlockSpec((None, tm, D_MODEL), lambda b, i: (b, i, 0)),
        out_shape=jax.ShapeDtypeStruct((bn, s, D_MODEL), F32),
        scratch_shapes=[pltpu.VMEM((tm, D_MODEL), F32)],
        compiler_params=_cparams(("arbitrary", "arbitrary")),
        name="ffn_final" if final else "ffn",
    )(x, x, x, mod, norm_g, w_up, conv_w, w_down, final_g)


def _prepare(w_in, m_gate_b, m_norm_g, dn_a_log, dn_dt_bias, dn_norm_g):
    bounds = [0]
    for cw in COLS:
        bounds.append(bounds[-1] + cw)
    seg = [w_in[:, :, bounds[j]:bounds[j + 1]] for j in range(len(COLS))]
    m_q, m_k, m_v, m_o, m_if, d_qkv, d_z, d_ba, c_bcx, br_g = seg
    w_main = jnp.concatenate([m_q * (M_DQK ** -0.5), m_k, m_v, m_o, d_qkv, d_z, c_bcx, br_g], axis=-1).astype(BF16)
    pad = jnp.zeros(w_in.shape[:2] + (LANE - 4 * M_HEADS - 4 * D_HEADS,), w_in.dtype)
    w_gate = jnp.concatenate([m_if, d_ba, pad], axis=-1).astype(BF16)
    w_gate_t = jnp.swapaxes(w_gate, 1, 2)

    lanes = lambda t, off: jnp.pad(t, ((0, 0), (off, LANE - off - t.shape[1])))
    bias = lanes(m_gate_b, GATE_M)
    neg_a = -jnp.exp(dn_a_log.astype(F32))
    zeros_h = jnp.zeros((DEPTH, D_HEADS), F32)
    a_l = lanes(jnp.concatenate([zeros_h, neg_a[:, 0], zeros_h, neg_a[:, 1]], axis=-1), GATE_D)
    dtb = dn_dt_bias.astype(F32)
    dt_l = lanes(jnp.concatenate([zeros_h, dtb[:, 0], zeros_h, dtb[:, 1]], axis=-1), GATE_D)
    as_row = lambda t: t[:, None, :]
    as_col = lambda t: t[:, :, None]
    return dict(
        w_main=w_main, w_gate=w_gate, w_gate_t=w_gate_t,
        bias_r=as_row(bias), bias_c=as_col(bias),
        a_r=as_row(a_l), a_c=as_col(a_l), dt_r=as_row(dt_l), dt_c=as_col(dt_l),
        m_norm=jnp.tile(m_norm_g, (1, M_HEADS))[:, None, :], dn_norm=jnp.tile(dn_norm_g, (1, D_HEADS))[:, None, :],
    )


def _trunk(x, c, prm, norm1_g, w_ada, b_ada, dn_conv_w, cm_conv_w, w_br_m, w_br_d, w_br_c, w_out, norm2_g,
           w_up, ffn_conv_w, w_down, final_g):
    mod = _modulation(c, w_ada, b_ada)
    fg = final_g[None, :]
    for l in range(DEPTH):
        p, g, gt = _proj_in(x, mod[l], norm1_g[l][None, :], prm["w_main"][l], prm["w_gate"][l], prm["w_gate_t"][l])
        hf, hb = _mlstm(p, g, gt, prm["bias_r"][l], prm["bias_c"][l])
        prep = _gdn_prep(p, g, gt, dn_conv_w[l], prm["a_r"][l], prm["a_c"][l], prm["dt_r"][l], prm["dt_c"][l])
        of, ob = _gdn_scan(prep)
        x = _merge(x, hf, hb, of, ob, p, mod[l], prm["m_norm"][l], prm["dn_norm"][l], cm_conv_w[l],
                   w_br_m[l], w_br_d[l], w_br_c[l], w_out[l])
        x = _ffn(x, mod[l], norm2_g[l][None, :], w_up[l], ffn_conv_w[l], w_down[l], fg, l == DEPTH - 1)
    return x


def kernel(x_prompt, x_sample, c_prompt, c_sample, norm1_g, w_ada, b_ada, w_in, m_gate_b, m_norm_g, dn_conv_w, dn_a_log, dn_dt_bias, dn_norm_g, cm_conv_w, w_br_m, w_br_d, w_br_c, w_out, norm2_g, w_up, ffn_conv_w, w_down, final_g):
    prm = _prepare(w_in, m_gate_b, m_norm_g, dn_a_log, dn_dt_bias, dn_norm_g)
    bf = lambda t: t.astype(BF16)
    args = (prm, norm1_g, w_ada, b_ada, dn_conv_w, cm_conv_w, bf(w_br_m), bf(w_br_d), bf(w_br_c), bf(w_out),
            norm2_g, bf(w_up), ffn_conv_w, bf(w_down), final_g)
    return (_trunk(x_prompt, c_prompt, *args), _trunk(x_sample, c_sample, *args))
```

```python
import functools

import jax
import jax.numpy as jnp
from jax import lax
from jax.experimental import pallas as pl
from jax.experimental.pallas import tpu as pltpu

F32 = jnp.float32
BF16 = jnp.bfloat16

D_MODEL = 1024
DEPTH = 4
M_HEADS = 4
M_DQK = 128
M_DV = 256
M_CHUNK = 128
D_HEADS = 8
D_DK = 128
D_DV = 128
G_CHUNK = 128
D_FF = 2816
EPS = 1e-6

M_QK = M_HEADS * M_DQK
M_V = M_HEADS * M_DV
D_QK = D_HEADS * D_DK
D_V = D_HEADS * D_DV
COLS = (M_QK, M_QK, M_V, M_V, 4 * M_HEADS, 2 * D_QK + D_V, D_V, 4 * D_HEADS, 3 * D_MODEL, 3 * D_MODEL)

LANE = 128
P_MAIN = 13 * 1024
BLK_MO, BLK_DZ, BLK_CB, BLK_CC, BLK_CX, BLK_GA, BLK_GB, BLK_GC = 2, 6, 7, 8, 9, 10, 11, 12
GATE_M = 0
GATE_D = 4 * M_HEADS

M_BATCH = 2
P_DTYPE = BF16
P_HALO = 16
VMEM_LIMIT = 56 * 1024 * 1024


def _cparams(sem):
    return pltpu.CompilerParams(dimension_semantics=sem, vmem_limit_bytes=VMEM_LIMIT)


def _sigmoid(x):
    return 0.5 * jnp.tanh(0.5 * x) + 0.5


def _silu(x):
    return x * _sigmoid(x)


def _softplus(x):
    return jnp.maximum(x, 0.0) + jnp.log(1.0 + jnp.exp(-jnp.abs(x)))


def _dot(a, b):
    return jnp.dot(a, b, preferred_element_type=F32)


def _dot_nt(a, b):
    return lax.dot_general(a, b, (((1,), (1,)), ((), ())), preferred_element_type=F32)


def _dot_tn(a, b):
    return lax.dot_general(a, b, (((0,), (0,)), ((), ())), preferred_element_type=F32)


def _dot_hi(a, b):
    return jnp.dot(a, b, preferred_element_type=F32, precision=lax.Precision.HIGHEST)


def _split3(x):
    h1 = x.astype(BF16)
    r1 = x - h1.astype(F32)
    h2 = r1.astype(BF16)
    h3 = (r1 - h2.astype(F32)).astype(BF16)
    return h1, h2, h3


def _cumsum_cols(mask01, x):
    h1, h2, h3 = _split3(x)
    return _dot(mask01, h1) + (_dot(mask01, h2) + _dot(mask01, h3))


def _cumsum_rows(x, mask01):
    h1, h2, h3 = _split3(x)
    return _dot(h1, mask01) + (_dot(h2, mask01) + _dot(h3, mask01))


SUBLANES = 8


def _shift_rows(x, prev_row, next_row):
    rows = x.shape[0]
    row = lax.broadcasted_iota(jnp.int32, (SUBLANES, x.shape[1]), 0)
    down = pltpu.roll(x, 1, 0)
    up = pltpu.roll(x, rows - 1, 0)
    down = jnp.concatenate([jnp.where(row == 0, prev_row, down[:SUBLANES]), down[SUBLANES:]], axis=0)
    up = jnp.concatenate([up[:rows - SUBLANES], jnp.where(row == SUBLANES - 1, next_row, up[rows - SUBLANES:])],
                         axis=0)
    return down, up


def _scan_masks(n, reverse):
    r = lax.broadcasted_iota(jnp.int32, (n, n), 0)
    c = lax.broadcasted_iota(jnp.int32, (n, n), 1)
    if reverse:
        return c >= r, c > r, r >= c
    return c <= r, c < r, r <= c


def _mod_kernel(c_ref, w_ref, b_ref, o_ref):
    o_ref[...] = _dot_hi(_silu(c_ref[...]), w_ref[...]) + b_ref[...]


def _modulation(c, w_ada, b_ada):
    bn = c.shape[0]
    out = pl.pallas_call(
        _mod_kernel,
        grid=(DEPTH, 6),
        in_specs=[
            pl.BlockSpec((bn, D_MODEL), lambda l, j: (0, 0)),
            pl.BlockSpec((None, D_MODEL, D_MODEL), lambda l, j: (l, 0, j)),
            pl.BlockSpec((None, 1, D_MODEL), lambda l, j: (l, 0, j)),
        ],
        out_specs=pl.BlockSpec((None, bn, D_MODEL), lambda l, j: (l, 0, j)),
        out_shape=jax.ShapeDtypeStruct((DEPTH, bn, 6 * D_MODEL), F32),
        compiler_params=_cparams(("arbitrary", "arbitrary")),
        name="adaln_mod",
    )(c, w_ada, b_ada.reshape(DEPTH, 1, 6 * D_MODEL))
    return out.reshape(DEPTH, bn, 6, D_MODEL)


def _norm_mod(x, g, scale, shift):
    y = x * lax.rsqrt(jnp.mean(x * x, axis=-1, keepdims=True) + EPS)
    return (y * g) * (1.0 + scale) + shift


PROJ_SUB = 256


def _proj_kernel(x_ref, mod_ref, g_ref, w_ref, wg_ref, wgt_ref, o_ref, og_ref, ogt_ref, h_ref):
    @pl.when(pl.program_id(2) == 0)
    def _():
        h = _norm_mod(x_ref[...], g_ref[...], mod_ref[1:2, :], mod_ref[0:1, :]).astype(BF16)
        h_ref[...] = h
        og_ref[...] = _dot(h, wg_ref[...])
        ogt_ref[...] = _dot_nt(wgt_ref[...], h)

    h = h_ref[...]
    for c in range(o_ref.shape[1] // PROJ_SUB):
        cs = slice(c * PROJ_SUB, (c + 1) * PROJ_SUB)
        o_ref[:, cs] = _dot(h, w_ref[:, cs]).astype(o_ref.dtype)


def _proj_in(x, mod, norm_g, w_main, w_gate, w_gate_t):
    bn, s, _ = x.shape
    tm = min(s, 1024)
    tn = P_MAIN // 4
    return pl.pallas_call(
        _proj_kernel,
        grid=(bn, s // tm, P_MAIN // tn),
        in_specs=[
            pl.BlockSpec((None, tm, D_MODEL), lambda b, i, j: (b, i, 0)),
            pl.BlockSpec((None, 6, D_MODEL), lambda b, i, j: (b, 0, 0)),
            pl.BlockSpec((1, D_MODEL), lambda b, i, j: (0, 0)),
            pl.BlockSpec((D_MODEL, tn), lambda b, i, j: (0, j)),
            pl.BlockSpec((D_MODEL, LANE), lambda b, i, j: (0, 0)),
            pl.BlockSpec((LANE, D_MODEL), lambda b, i, j: (0, 0)),
        ],
        out_specs=[
            pl.BlockSpec((None, tm, tn), lambda b, i, j: (b, i, j)),
            pl.BlockSpec((None, tm, LANE), lambda b, i, j: (b, i, 0)),
            pl.BlockSpec((None, LANE, tm), lambda b, i, j: (b, 0, i)),
        ],
        out_shape=[
            jax.ShapeDtypeStruct((bn, s, P_MAIN), P_DTYPE),
            jax.ShapeDtypeStruct((bn, s, LANE), F32),
            jax.ShapeDtypeStruct((bn, LANE, s), F32),
        ],
        scratch_shapes=[pltpu.VMEM((tm, D_MODEL), BF16)],
        compiler_params=_cparams(("arbitrary", "arbitrary", "arbitrary")),
        name="proj_in",
    )(x, mod, norm_g, w_main, w_gate, w_gate_t)


def _log_sigmoid(x):
    return jnp.minimum(x, 0.0) - jnp.log(1.0 + jnp.exp(-jnp.abs(x)))


def _mlstm_kernel(qf_ref, kf_ref, vf_ref, gf_ref, gtf_ref, qb_ref, kb_ref, vb_ref, gb_ref, gtb_ref,
                  bias_r_ref, bias_c_ref, hf_ref, hb_ref, c_ref, m_ref):
    @pl.when(pl.program_id(1) == 0)
    def _():
        c_ref[...] = jnp.zeros_like(c_ref)
        m_ref[...] = jnp.zeros_like(m_ref)


    L = M_CHUNK
    dirs = (
        (qf_ref, kf_ref, vf_ref, gf_ref, gtf_ref, hf_ref, False, L - 1, 0),
        (qb_ref, kb_ref, vb_ref, gb_ref, gtb_ref, hb_ref, True, 0, 2),
    )
    units = []
    for bi, d in [(bi, d) for bi in range(M_BATCH) for d in range(2)]:
        q_ref, k_ref, v_ref, g_ref, gt_ref, h_ref, reverse, last, gbase = dirs[d]
        mask, _, mask_t = _scan_masks(L, reverse)
        gc = g_ref[bi] + bias_r_ref[...]
        gr = gt_ref[bi] + bias_c_ref[...]
        bc_all = _cumsum_cols(mask.astype(BF16), _log_sigmoid(gc))
        br_all = _cumsum_rows(_log_sigmoid(gr), mask_t.astype(BF16))
        for h in range(M_HEADS):
            ci = GATE_M + gbase * M_HEADS + h
            cf = ci + M_HEADS
            idx = (bi * 2 + d) * M_HEADS + h
            k = k_ref[bi, :, h * M_DQK:(h + 1) * M_DQK].astype(F32)
            vb = jnp.concatenate([v_ref[bi, :, h * M_DV:(h + 1) * M_DV].astype(BF16), jnp.ones((L, LANE), BF16)],
                                 axis=1)
            units.append(dict(
                idx=idx, mask=mask, last=last, h_ref=h_ref, bi=bi, h=h,
                b_col=bc_all[:, cf:cf + 1], b_row=br_all[cf:cf + 1, :], i_col=gc[:, ci:ci + 1], i_row=gr[ci:ci + 1, :],
                m_old=m_ref[idx:idx + 1, 0:1], c_old=c_ref[idx],
                k=k, qb=q_ref[bi, :, h * M_DQK:(h + 1) * M_DQK].astype(BF16), vb=vb))

    for u in units:
        u["qk"] = _dot_nt(u["qb"], u["k"].astype(BF16))
        u["qc"] = _dot(u["qb"], u["c_old"].astype(BF16))
    for u in units:
        u["dm"] = jnp.where(u["mask"], u["b_col"] - u["b_row"] + u["i_row"], -jnp.inf)
    for u in units:
        u["rowmax"] = jnp.max(u["dm"], axis=-1, keepdims=True)
    for u in units:
        a_col = u["b_col"] + u["m_old"]
        u["mt"] = jnp.maximum(a_col, u["rowmax"])
        u["sa"] = jnp.exp(a_col - u["mt"])
    for u in units:
        u["floor"] = jnp.exp(-jnp.broadcast_to(u["mt"], (L, LANE)))
        u["s"] = u["qk"] * jnp.exp(u["dm"] - u["mt"])
    for u in units:
        u["sv"] = _dot(u["s"].astype(BF16), u["vb"])
    for u in units:
        u["num"] = u["sa"] * u["qc"] + u["sv"]
    for u in units:
        num = u["num"]
        inv = 1.0 / jnp.maximum(jnp.abs(num[:, M_DV:]), u["floor"])
        for t in range(M_DV // LANE):
            lo = u["h"] * M_DV + t * LANE
            u["h_ref"][u["bi"], :, lo:lo + LANE] = (num[:, t * LANE:(t + 1) * LANE] * inv).astype(u["h_ref"].dtype)
    for u in units:
        last = u["last"]
        b_tot = u["b_col"][last:last + 1, :]
        lw_row = b_tot - u["b_row"] + u["i_row"]
        u["lw_col"] = b_tot - u["b_col"] + u["i_col"]
        u["bm"] = b_tot + u["m_old"]
        u["lwmax"] = jnp.max(lw_row, axis=-1, keepdims=True)
    for u in units:
        u["m_new"] = jnp.maximum(u["bm"], u["lwmax"])
        u["wk"] = jnp.exp(u["lw_col"] - u["m_new"])
    for u in units:
        u["wkk"] = (u["wk"] * u["k"]).astype(BF16)
    for u in units:
        idx = u["idx"]
        dec = jnp.exp(u["bm"] - u["m_new"])
        c_ref[idx] = dec * u["c_old"] + _dot_tn(u["wkk"], u["vb"])
        m_ref[idx:idx + 1, :] = jnp.broadcast_to(u["m_new"], (1, LANE))


def _mlstm(p, g, gt, bias_r, bias_c):
    bn, s, _ = p.shape
    L = M_CHUNK
    n = s // L
    mb = M_BATCH
    blk = lambda w, j, rev: pl.BlockSpec((mb, L, w), (lambda b, i: (b, n - 1 - i, j)) if rev else (lambda b, i: (b, i, j)))
    gtblk = lambda rev: pl.BlockSpec((mb, LANE, L), (lambda b, i: (b, 0, n - 1 - i)) if rev else (lambda b, i: (b, 0, i)))
    in_specs = []
    for rev in (False, True):
        in_specs += [blk(M_QK, 0, rev), blk(M_QK, 1, rev), blk(M_V, 1, rev), blk(LANE, 0, rev), gtblk(rev)]
    in_specs += [pl.BlockSpec((1, LANE), lambda b, i: (0, 0)), pl.BlockSpec((LANE, 1), lambda b, i: (0, 0))]
    n_units = mb * 2 * M_HEADS
    return pl.pallas_call(
        _mlstm_kernel,
        grid=(bn // mb, n),
        in_specs=in_specs,
        out_specs=[blk(M_V, 0, False), blk(M_V, 0, True)],
        out_shape=[jax.ShapeDtypeStruct((bn, s, M_V), P_DTYPE)] * 2,
        scratch_shapes=[
            pltpu.VMEM((n_units, M_DQK, M_DV + LANE), F32),
            pltpu.VMEM((n_units, LANE), F32),
        ],
        compiler_params=_cparams(("arbitrary", "arbitrary")),
        name="mlstm",
    )(p, p, p, g, gt, p, p, p, g, gt, bias_r, bias_c)


G_SCAN_BATCH = 2
G_PREP_CHUNKS = 2
INV_BASE = 8


def _lane_pair(a, b):
    return jnp.concatenate([a, b], axis=1)


def _blockdiag(pair):
    n = pair.shape[0]
    z = jnp.zeros((n, n), pair.dtype)
    return jnp.concatenate([_lane_pair(pair[:, :n], z), _lane_pair(z, pair[:, n:])], axis=0)


def _pair_dot(lhs, rhs):
    return _dot(lhs, _blockdiag(rhs))


def _unit_tri_inverse_pairs(a_list):
    n = a_list[0].shape[0]
    r = lax.broadcasted_iota(jnp.int32, (n, 2 * n), 0)
    c = lax.broadcasted_iota(jnp.int32, (n, 2 * n), 1) % n
    same_block = lambda size: (r // size) == (c // size)
    eye = (r == c).astype(F32)
    bf = lambda xs: [x.astype(BF16) for x in xs]
    a0 = [jnp.where(same_block(INV_BASE), a, 0.0) for a in a_list]
    a0b = bf(a0)
    a2 = [_pair_dot(x, x) for x in a0b]
    a2b = bf(a2)
    a3 = [_pair_dot(x2, x) for x2, x in zip(a2b, a0b)]
    a4 = [_pair_dot(x2, x2) for x2 in a2b]
    p = [eye - x + x2 - x3 for x, x2, x3 in zip(a0, a2, a3)]
    t = [pp + _pair_dot(pb, x4) for pp, pb, x4 in zip(p, bf(p), bf(a4))]
    ab = bf(a_list)
    size = INV_BASE
    while size < n:
        level = jnp.logical_and(same_block(2 * size), jnp.logical_not(same_block(size))).astype(BF16)
        tb = bf(t)
        x = [_pair_dot(a * level, tt) for a, tt in zip(ab, tb)]
        t = [tt - _pair_dot(tq, xx) for tt, tq, xx in zip(t, tb, bf(x))]
        size *= 2
    return t


def _gdn_prep_kernel(x_ref, xp_ref, xn_ref, g_ref, gt_ref, cw_ref, ar_ref, ac_ref, dr_ref, dc_ref,
                     v_ref, kgf_ref, qgf_ref, kdf_ref, tf_ref, qkf_ref, egf_ref,
                     kgb_ref, qgb_ref, kdb_ref, tb_ref, qkb_ref, egb_ref):
    i = pl.program_id(1)
    nblk = pl.num_programs(1)
    L = G_CHUNK
    n_chunks = x_ref.shape[0] // L
    gc_all = g_ref[...]
    gr_all = gt_ref[...]
    beta_c_all = _sigmoid(gc_all)
    beta_r_all = _sigmoid(gr_all)
    lg_c_all = ar_ref[...] * _softplus(gc_all + dr_ref[...])
    lg_r_all = ac_ref[...] * _softplus(gr_all + dc_ref[...])
    row_of = lambda ref, r: ref[r:r + 1, :].astype(F32)

    for ck in range(n_chunks):
        rs = slice(ck * L, (ck + 1) * L)
        x = x_ref[rs, :].astype(F32)
        prev_row = jnp.where(i > 0, row_of(xp_ref, P_HALO - 1), 0.0) if ck == 0 else row_of(x_ref, ck * L - 1)
        next_row = (jnp.where(i < nblk - 1, row_of(xn_ref, 0), 0.0) if ck == n_chunks - 1
                    else row_of(x_ref, (ck + 1) * L))
        xm, xq = _shift_rows(x, prev_row, next_row)
        y = _silu(xm * cw_ref[0:1, :] + x * cw_ref[1:2, :] + xq * cw_ref[2:3, :])
        v_ref[rs, :] = y[:, 2 * D_QK:].astype(v_ref.dtype)

        units = []
        amats = []
        beta_c, beta_r, lg_c, lg_r = beta_c_all[rs], beta_r_all[:, rs], lg_c_all[rs], lg_r_all[:, rs]
        qn, kn = [], []
        for h in range(D_HEADS):
            qh = y[:, h * D_DK:(h + 1) * D_DK]
            kh = y[:, D_QK + h * D_DK:D_QK + (h + 1) * D_DK]
            qn.append(qh * lax.rsqrt(jnp.sum(qh * qh, axis=-1, keepdims=True) + EPS) * (D_DK ** -0.5))
            kn.append(kh * lax.rsqrt(jnp.sum(kh * kh, axis=-1, keepdims=True) + EPS))
        kk_p, qk_p = [], []
        for hp in range(D_HEADS // 2):
            kp = _lane_pair(kn[2 * hp], kn[2 * hp + 1]).astype(BF16)
            qp = _lane_pair(qn[2 * hp], qn[2 * hp + 1]).astype(BF16)
            kbd = _blockdiag(kp)
            kk_p.append(_dot_nt(kp, kbd))
            qk_p.append(_dot_nt(qp, kbd))

        for reverse, last, gbase, refs in (
            (False, L - 1, 0, (kgf_ref, qgf_ref, kdf_ref, tf_ref, qkf_ref, egf_ref)),
            (True, 0, 2, (kgb_ref, qgb_ref, kdb_ref, tb_ref, qkb_ref, egb_ref)),
        ):
            mask, strict, mask_t = _scan_masks(L, reverse)
            cum_c = _cumsum_cols(mask.astype(BF16), lg_c)
            cum_r = _cumsum_rows(lg_r, mask_t.astype(BF16))
            kg_ref, qg_ref, kd_ref, t_ref, qkd_ref, eg_ref = refs
            for hp in range(D_HEADS // 2):
                decays, a_halves, b_rows = [], [], []
                for half, h in enumerate((2 * hp, 2 * hp + 1)):
                    cb = GATE_D + gbase * D_HEADS + h
                    ca = cb + D_HEADS
                    gcum_c = cum_c[:, ca:ca + 1]
                    decay = jnp.exp(jnp.where(mask, gcum_c - cum_r[ca:ca + 1, :], -jnp.inf))
                    kk = kk_p[hp][:, half * L:(half + 1) * L]
                    a_halves.append(jnp.where(strict, (beta_c[:, cb:cb + 1] * kk) * decay, 0.0))
                    decays.append(decay)
                    b_rows.append(beta_r[cb:cb + 1, :])
                    g_tot = gcum_c[last:last + 1, :]
                    eg = jnp.exp(gcum_c)
                    hs = slice(h * D_DK, (h + 1) * D_DK)
                    kg_ref[rs, hs] = (kn[h] * eg).astype(kg_ref.dtype)
                    qg_ref[rs, hs] = (qn[h] * eg).astype(qg_ref.dtype)
                    kd_ref[rs, hs] = (kn[h] * jnp.exp(g_tot - gcum_c)).astype(kd_ref.dtype)
                    er = ck * D_HEADS + h
                    eg_ref[er:er + 1, :] = jnp.broadcast_to(jnp.exp(g_tot), (1, LANE))
                ps = slice(hp * 2 * L, (hp + 1) * 2 * L)
                qkd_ref[rs, ps] = (qk_p[hp] * _lane_pair(*decays)).astype(qkd_ref.dtype)
                amats.append(_lane_pair(*a_halves))
                units.append((t_ref, ps, _lane_pair(*b_rows)))

        for (t_ref, ps, b_row), tmat in zip(units, _unit_tri_inverse_pairs(amats)):
            t_ref[rs, ps] = (tmat * b_row).astype(t_ref.dtype)


def _gdn_prep(p, g, gt, conv_w, a_row, a_col, dt_row, dt_col):
    bn, s, _ = p.shape
    L = G_CHUNK
    n = s // L
    w3 = 2 * D_QK + D_V
    nck = min(G_PREP_CHUNKS, n)
    rows = nck * L
    rh = rows // P_HALO
    tok = lambda w: pl.BlockSpec((None, rows, w), lambda b, i: (b, i, 0))
    const = lambda shape: pl.BlockSpec(shape, lambda b, i: (0,) * len(shape))
    big = jax.ShapeDtypeStruct((bn, s, D_QK), BF16)
    egs = jax.ShapeDtypeStruct((bn, n * D_HEADS, LANE), F32)
    per_dir_specs = [tok(D_QK)] * 5 + [pl.BlockSpec((None, nck * D_HEADS, LANE), lambda b, i: (b, i, 0))]
    per_dir_shapes = [big] * 5 + [egs]
    return pl.pallas_call(
        _gdn_prep_kernel,
        grid=(bn, n // nck),
        in_specs=[
            pl.BlockSpec((None, rows, w3), lambda b, i: (b, i, 1)),
            pl.BlockSpec((None, P_HALO, w3), lambda b, i: (b, jnp.maximum(i * rh - 1, 0), 1)),
            pl.BlockSpec((None, P_HALO, w3), lambda b, i: (b, jnp.minimum((i + 1) * rh, s // P_HALO - 1), 1)),
            pl.BlockSpec((None, rows, LANE), lambda b, i: (b, i, 0)),
            pl.BlockSpec((None, LANE, rows), lambda b, i: (b, 0, i)),
            const((3, w3)), const((1, LANE)), const((LANE, 1)), const((1, LANE)), const((LANE, 1)),
        ],
        out_specs=[tok(D_V)] + per_dir_specs * 2,
        out_shape=[big] + per_dir_shapes * 2,
        compiler_params=_cparams(("arbitrary", "arbitrary")),
        name="gdn_prep",
    )(p, p, p, g, gt, conv_w, a_row, a_col, dt_row, dt_col)


def _gdn_scan_kernel(vf_ref, kgf_ref, qgf_ref, kdf_ref, tf_ref, qkf_ref, egf_ref,
                     vb_ref, kgb_ref, qgb_ref, kdb_ref, tb_ref, qkb_ref, egb_ref,
                     of_ref, ob_ref, s_ref):
    @pl.when(pl.program_id(1) == 0)
    def _():
        s_ref[...] = jnp.zeros_like(s_ref)

    L = G_CHUNK
    units = []
    for bi in range(vf_ref.shape[0]):
        for d, refs in enumerate((
            (vf_ref, kgf_ref, qgf_ref, kdf_ref, tf_ref, qkf_ref, egf_ref, of_ref),
            (vb_ref, kgb_ref, qgb_ref, kdb_ref, tb_ref, qkb_ref, egb_ref, ob_ref),
        )):
            for hp in range(D_HEADS // 2):
                idx = (bi * 2 + d) * D_HEADS + 2 * hp
                units.append((bi, idx, 2 * hp, slice(hp * 2 * L, (hp + 1) * 2 * L)) + refs)

    xs = []
    for bi, idx, h0, ps, v_ref, kg_ref, qg_ref, kd_ref, t_ref, qkd_ref, eg_ref, o_ref in units:
        s_bd = _blockdiag(_lane_pair(s_ref[idx].astype(BF16), s_ref[idx + 1].astype(BF16)))
        xs.append(_dot(jnp.concatenate([kg_ref[bi, :, ps], qg_ref[bi, :, ps]], axis=0), s_bd))
    vns = []
    for x, (bi, idx, h0, ps, v_ref, kg_ref, qg_ref, kd_ref, t_ref, qkd_ref, eg_ref, o_ref) in zip(xs, units):
        rhs = (v_ref[bi, :, ps].astype(F32) - x[:L]).astype(BF16)
        vns.append(_pair_dot(t_ref[bi, :, ps], rhs).astype(BF16))
    for x, vn, (bi, idx, h0, ps, v_ref, kg_ref, qg_ref, kd_ref, t_ref, qkd_ref, eg_ref, o_ref) in zip(xs, vns, units):
        o_ref[bi, :, ps] = (x[L:] + _pair_dot(qkd_ref[bi, :, ps], vn)).astype(o_ref.dtype)
        for half in range(2):
            hs = slice((h0 + half) * D_DK, (h0 + half + 1) * D_DK)
            s_ref[idx + half] = (eg_ref[bi, h0 + half:h0 + half + 1, :] * s_ref[idx + half]
                                 + _dot_tn(kd_ref[bi, :, hs], vn[:, half * L:(half + 1) * L]))


def _gdn_scan(prep):
    v = prep[0]
    bn, s, _ = v.shape
    L = G_CHUNK
    n = s // L

    sb = G_SCAN_BATCH

    def specs(rev):
        im = (lambda b, i: (b, n - 1 - i, 0)) if rev else (lambda b, i: (b, i, 0))
        return [pl.BlockSpec((sb, L, D_QK), im)] * 6 + [pl.BlockSpec((sb, D_HEADS, LANE), im)]

    fwd_ops = [v] + list(prep[1:7])
    bwd_ops = [v] + list(prep[7:13])
    return pl.pallas_call(
        _gdn_scan_kernel,
        grid=(bn // sb, n),
        in_specs=specs(False) + specs(True),
        out_specs=[pl.BlockSpec((sb, L, D_V), lambda b, i: (b, i, 0)),
                   pl.BlockSpec((sb, L, D_V), lambda b, i: (b, n - 1 - i, 0))],
        out_shape=[jax.ShapeDtypeStruct((bn, s, D_V), P_DTYPE)] * 2,
        scratch_shapes=[pltpu.VMEM((sb * 2 * D_HEADS, D_DK, D_DV), F32)],
        compiler_params=_cparams(("arbitrary", "arbitrary")),
        name="gdn_scan",
    )(*fwd_ops, *bwd_ops)


def _head_rmsnorm(x, g, width):
    outs = []
    for h in range(x.shape[1] // width):
        xs = x[:, h * width:(h + 1) * width]
        outs.append(xs * lax.rsqrt(jnp.mean(xs * xs, axis=-1, keepdims=True) + EPS))
    return jnp.concatenate(outs, axis=1) * g


def _merge_kernel(x_ref, hf_ref, hb_ref, of_ref, ob_ref, mo_ref, dz_ref, cb_ref, cc_ref, cx_ref,
                  ga_ref, gb_ref, gc_ref, ccp_ref, cxp_ref, ccn_ref, cxn_ref, mod_ref,
                  mg_ref, dg_ref, cw_ref, wm_ref, wd_ref, wc_ref, wo_ref, o_ref):
    i = pl.program_id(1)
    nblk = pl.num_programs(1)
    f = lambda ref: ref[...].astype(F32)
    hm = _head_rmsnorm(f(hf_ref) + f(hb_ref), mg_ref[...], M_DV)
    out_a = _dot(_sigmoid(mo_ref[...].astype(BF16)) * hm.astype(BF16), wm_ref[...])
    od = _head_rmsnorm(f(of_ref) + f(ob_ref), dg_ref[...], D_DV)
    out_b = _dot(od.astype(BF16) * _silu(dz_ref[...].astype(BF16)), wd_ref[...])

    u = f(cc_ref) * f(cx_ref)
    prev_row = jnp.where(i > 0, f(ccp_ref)[P_HALO - 1:P_HALO, :] * f(cxp_ref)[P_HALO - 1:P_HALO, :], 0.0)
    next_row = jnp.where(i < nblk - 1, f(ccn_ref)[0:1, :] * f(cxn_ref)[0:1, :], 0.0)
    um, up = _shift_rows(u, prev_row, next_row)
    conv = um * cw_ref[0:1, :] + u * cw_ref[1:2, :] + up * cw_ref[2:3, :]
    out_c = _dot(cb_ref[...].astype(BF16) * conv.astype(BF16), wc_ref[...])

    gate = lambda ref, out: _sigmoid(ref[...].astype(BF16)) * out.astype(BF16)
    merged = gate(ga_ref, out_a) + gate(gb_ref, out_b) + gate(gc_ref, out_c)
    o_ref[...] = x_ref[...] + mod_ref[2:3, :] * _dot(merged, wo_ref[...])


def _merge(x, hf, hb, of, ob, p, mod, m_norm_g, dn_norm_g, cm_conv_w, w_br_m, w_br_d, w_br_c, w_out):
    bn, s, _ = x.shape
    tm = min(s, 256)
    rh = tm // P_HALO
    tok = pl.BlockSpec((None, tm, D_MODEL), lambda b, i: (b, i, 0))
    pblk = lambda j: pl.BlockSpec((None, tm, D_MODEL), lambda b, i: (b, i, j))
    prev = lambda j: pl.BlockSpec((None, P_HALO, D_MODEL), lambda b, i: (b, jnp.maximum(i * rh - 1, 0), j))
    nxt = lambda j: pl.BlockSpec((None, P_HALO, D_MODEL),
                                 lambda b, i: (b, jnp.minimum((i + 1) * rh, s // P_HALO - 1), j))
    const = lambda shape: pl.BlockSpec(shape, lambda b, i: (0,) * len(shape))
    wspec = const((D_MODEL, D_MODEL))
    return pl.pallas_call(
        _merge_kernel,
        grid=(bn, s // tm),
        in_specs=[tok] * 5 + [pblk(j) for j in (BLK_MO, BLK_DZ, BLK_CB, BLK_CC, BLK_CX, BLK_GA, BLK_GB, BLK_GC)]
        + [prev(BLK_CC), prev(BLK_CX), nxt(BLK_CC), nxt(BLK_CX)]
        + [pl.BlockSpec((None, 6, D_MODEL), lambda b, i: (b, 0, 0))]
        + [const((1, D_MODEL)), const((1, D_MODEL)), const((3, D_MODEL))] + [wspec] * 4,
        out_specs=tok,
        out_shape=jax.ShapeDtypeStruct((bn, s, D_MODEL), F32),
        compiler_params=_cparams(("arbitrary", "arbitrary")),
        name="merge",
    )(x, hf, hb, of, ob, p, p, p, p, p, p, p, p, p, p, p, p, mod, m_norm_g, dn_norm_g, cm_conv_w,
      w_br_m, w_br_d, w_br_c, w_out)


FF_TILE = 256


def _ffn_kernel(x_ref, xp_ref, xn_ref, mod_ref, g_ref, wu_ref, cw_ref, wd_ref, fg_ref, o_ref, acc_ref, *, final):
    i = pl.program_id(1)
    nblk = pl.num_programs(1)
    x = x_ref[...]
    tm = x.shape[0]
    xa = jnp.concatenate([xp_ref[...], x, xn_ref[...]], axis=0)
    hf = _norm_mod(xa, g_ref[...], mod_ref[4:5, :], mod_ref[3:4, :])
    h = hf.astype(BF16)
    hmid = hf[8:8 + tm].astype(BF16)
    rows = tm + 16
    row = lax.broadcasted_iota(jnp.int32, (rows, FF_TILE), 0)
    kill_prev = jnp.logical_and(row == 7, i == 0)
    kill_next = jnp.logical_and(row == tm + 8, i == nblk - 1)
    kill = jnp.logical_or(kill_prev, kill_next)
    n_tiles = D_FF // FF_TILE

    def up(c):
        a = _dot(h, wu_ref[:, c * FF_TILE:(c + 1) * FF_TILE])
        b = _dot(hmid, wu_ref[:, D_FF + c * FF_TILE:D_FF + (c + 1) * FF_TILE])
        return a, b

    nxt = up(0)
    for c in range(n_tiles):
        cs = slice(c * FF_TILE, (c + 1) * FF_TILE)
        a, b = nxt
        if c + 1 < n_tiles:
            nxt = up(c + 1)
        a = jnp.where(kill, 0.0, a)
        conv = (pltpu.roll(a, 1, 0) * cw_ref[0:1, cs] + a * cw_ref[1:2, cs]
                + pltpu.roll(a, rows - 1, 0) * cw_ref[2:3, cs])[8:8 + tm]
        y = (_silu(conv) * b).astype(BF16)
        contrib = _dot(y, wd_ref[cs, :])
        if c == 0:
            acc_ref[...] = contrib
        else:
            acc_ref[...] += contrib
    out = x + mod_ref[5:6, :] * acc_ref[...]
    if final:
        out = (out * lax.rsqrt(jnp.mean(out * out, axis=-1, keepdims=True) + EPS)) * fg_ref[...]
    o_ref[...] = out


def _ffn(x, mod, norm_g, w_up, conv_w, w_down, final_g, final):
    bn, s, _ = x.shape
    tm = min(s, 512)
    r8 = tm // 8
    const = lambda shape: pl.BlockSpec(shape, lambda b, i: (0,) * len(shape))
    return pl.pallas_call(
        functools.partial(_ffn_kernel, final=final),
        grid=(bn, s // tm),
        in_specs=[
            pl.BlockSpec((None, tm, D_MODEL), lambda b, i: (b, i, 0)),
            pl.BlockSpec((None, 8, D_MODEL), lambda b, i: (b, jnp.maximum(i * r8 - 1, 0), 0)),
            pl.BlockSpec((None, 8, D_MODEL), lambda b, i: (b, jnp.minimum((i + 1) * r8, s // 8 - 1), 0)),
            pl.BlockSpec((None, 6, D_MODEL), lambda b, i: (b, 0, 0)),
            const((1, D_MODEL)), const((D_MODEL, 2 * D_FF)), const((3, D_FF)), const((D_FF, D_MODEL)),
            const((1, D_MODEL)),
        ],
        out_specs=pl.BlockSpec((None, tm, D_MODEL), lambda b, i: (b, i, 0)),
        out_shape=jax.ShapeDtypeStruct((bn, s, D_MODEL), F32),
        scratch_shapes=[pltpu.VMEM((tm, D_MODEL), F32)],
        compiler_params=_cparams(("arbitrary", "arbitrary")),
        name="ffn_final" if final else "ffn",
    )(x, x, x, mod, norm_g, w_up, conv_w, w_down, final_g)


def _prepare(w_in, m_gate_b, m_norm_g, dn_a_log, dn_dt_bias, dn_norm_g):
    bounds = [0]
    for cw in COLS:
        bounds.append(bounds[-1] + cw)
    seg = [w_in[:, :, bounds[j]:bounds[j + 1]] for j in range(len(COLS))]
    m_q, m_k, m_v, m_o, m_if, d_qkv, d_z, d_ba, c_bcx, br_g = seg
    w_main = jnp.concatenate([m_q * (M_DQK ** -0.5), m_k, m_v, m_o, d_qkv, d_z, c_bcx, br_g], axis=-1).astype(BF16)
    pad = jnp.zeros(w_in.shape[:2] + (LANE - 4 * M_HEADS - 4 * D_HEADS,), w_in.dtype)
    w_gate = jnp.concatenate([m_if, d_ba, pad], axis=-1).astype(BF16)
    w_gate_t = jnp.swapaxes(w_gate, 1, 2)

    lanes = lambda t, off: jnp.pad(t, ((0, 0), (off, LANE - off - t.shape[1])))
    bias = lanes(m_gate_b, GATE_M)
    neg_a = -jnp.exp(dn_a_log.astype(F32))
    zeros_h = jnp.zeros((DEPTH, D_HEADS), F32)
    a_l = lanes(jnp.concatenate([zeros_h, neg_a[:, 0], zeros_h, neg_a[:, 1]], axis=-1), GATE_D)
    dtb = dn_dt_bias.astype(F32)
    dt_l = lanes(jnp.concatenate([zeros_h, dtb[:, 0], zeros_h, dtb[:, 1]], axis=-1), GATE_D)
    as_row = lambda t: t[:, None, :]
    as_col = lambda t: t[:, :, None]
    return dict(
        w_main=w_main, w_gate=w_gate, w_gate_t=w_gate_t,
        bias_r=as_row(bias), bias_c=as_col(bias),
        a_r=as_row(a_l), a_c=as_col(a_l), dt_r=as_row(dt_l), dt_c=as_col(dt_l),
        m_norm=jnp.tile(m_norm_g, (1, M_HEADS))[:, None, :], dn_norm=jnp.tile(dn_norm_g, (1, D_HEADS))[:, None, :],
    )


def _trunk(x, c, prm, norm1_g, w_ada, b_ada, dn_conv_w, cm_conv_w, w_br_m, w_br_d, w_br_c, w_out, norm2_g,
           w_up, ffn_conv_w, w_down, final_g):
    mod = _modulation(c, w_ada, b_ada)
    fg = final_g[None, :]
    for l in range(DEPTH):
        p, g, gt = _proj_in(x, mod[l], norm1_g[l][None, :], prm["w_main"][l], prm["w_gate"][l], prm["w_gate_t"][l])
        hf, hb = _mlstm(p, g, gt, prm["bias_r"][l], prm["bias_c"][l])
        prep = _gdn_prep(p, g, gt, dn_conv_w[l], prm["a_r"][l], prm["a_c"][l], prm["dt_r"][l], prm["dt_c"][l])
        of, ob = _gdn_scan(prep)
        x = _merge(x, hf, hb, of, ob, p, mod[l], prm["m_norm"][l], prm["dn_norm"][l], cm_conv_w[l],
                   w_br_m[l], w_br_d[l], w_br_c[l], w_out[l])
        x = _ffn(x, mod[l], norm2_g[l][None, :], w_up[l], ffn_conv_w[l], w_down[l], fg, l == DEPTH - 1)
    return x


def kernel(x_prompt, x_sample, c_prompt, c_sample, norm1_g, w_ada, b_ada, w_in, m_gate_b, m_norm_g, dn_conv_w, dn_a_log, dn_dt_bias, dn_norm_g, cm_conv_w, w_br_m, w_br_d, w_br_c, w_out, norm2_g, w_up, ffn_conv_w, w_down, final_g):
    prm = _prepare(w_in, m_gate_b, m_norm_g, dn_a_log, dn_dt_bias, dn_norm_g)
    bf = lambda t: t.astype(BF16)
    args = (prm, norm1_g, w_ada, b_ada, dn_conv_w, cm_conv_w, bf(w_br_m), bf(w_br_d), bf(w_br_c), bf(w_out),
            norm2_g, bf(w_up), ffn_conv_w, bf(w_down), final_g)
    return (_trunk(x_prompt, c_prompt, *args), _trunk(x_sample, c_sample, *args))
```
